```python
import jax, jax.numpy as jnp
from jax import lax
import numpy as np

D_MODEL = 4096
BATCH = 4
SEQ = 2048
DEPTH = 1

CHUNK = 64
MIX_WIDTH = D_MODEL
GDN_WIDTH = MIX_WIDTH // 2
SC_WIDTH = MIX_WIDTH - GDN_WIDTH
HEAD_DIM = 128
GDN_HEADS = GDN_WIDTH // HEAD_DIM
SC_GROUPS = SC_WIDTH // HEAD_DIM
GDN_CONV = 4
SC_CONV = 3
D_FF = 4 * D_MODEL
NORM_EPS = 1e-6
L2_EPS = 1e-6
IN_COLS = 4 * GDN_WIDTH + 2 * GDN_HEADS + 3 * SC_WIDTH

kernel_name = "hybrid_gdn_shortconv_sqrelu_block"


def rms_norm(x, w):
    xf = x.astype(jnp.float32)
    y = xf * lax.rsqrt(jnp.mean(xf * xf, axis=-1, keepdims=True) + NORM_EPS)
    return (y * w.astype(jnp.float32)).astype(x.dtype)


def l2_normalize(t):
    return t * lax.rsqrt(jnp.sum(t * t, axis=-1, keepdims=True) + L2_EPS)


def causal_depthwise_conv(u, w):
    k_width, ch = w.shape
    return lax.conv_general_dilated(
        u, w[:, None, :].astype(u.dtype), window_strides=(1,), padding=[(k_width - 1, 0)],
        dimension_numbers=("NWC", "WIO", "NWC"), feature_group_count=ch)


def gated_delta_rule_chunked(q, k, v, g, beta):
    bsz, seq, heads, dk = q.shape
    dv = v.shape[-1]
    n = seq // CHUNK
    q = l2_normalize(q.astype(jnp.float32)) * (dk ** -0.5)
    k = l2_normalize(k.astype(jnp.float32))
    v = v.astype(jnp.float32)

    def to_chunks(t):
        return t.reshape(bsz, n, CHUNK, heads, -1).transpose(0, 1, 3, 2, 4)

    q, k, v = to_chunks(q), to_chunks(k), to_chunks(v)
    g = jnp.cumsum(g.reshape(bsz, n, CHUNK, heads).transpose(0, 1, 3, 2), axis=-1)
    beta = beta.reshape(bsz, n, CHUNK, heads).transpose(0, 1, 3, 2)

    tri_incl = jnp.tril(jnp.ones((CHUNK, CHUNK), dtype=bool))
    tri_strict = jnp.tril(jnp.ones((CHUNK, CHUNK), dtype=bool), k=-1)
    decay = jnp.exp(jnp.where(tri_incl, g[..., :, None] - g[..., None, :], -jnp.inf))

    kk = jnp.einsum("bnhid,bnhjd->bnhij", k, k)
    lower = jnp.where(tri_strict, beta[..., None] * kk * decay, 0.0)
    a_mat = lower + jnp.eye(CHUNK, dtype=jnp.float32)
    rhs = jnp.concatenate([v * beta[..., None], k * (beta * jnp.exp(g))[..., None]], axis=-1)
    sol = lax.linalg.triangular_solve(a_mat, rhs, left_side=True, lower=True, unit_diagonal=True)
    u, w = sol[..., :dv], sol[..., dv:]

    qk = jnp.einsum("bnhid,bnhjd->bnhij", q, k) * decay
    q_dec = q * jnp.exp(g)[..., None]
    k_dec = k * jnp.exp(g[..., -1:] - g)[..., None]
    g_last = jnp.exp(g[..., -1])

    def step(state, inp):
        q_c, k_c, u_c, w_c, qk_c, gl_c = inp
        v_new = u_c - jnp.einsum("bhcd,bhde->bhce", w_c, state)
        o_c = (jnp.einsum("bhcd,bhde->bhce", q_c, state)
               + jnp.einsum("bhij,bhje->bhie", qk_c, v_new))
        state = state * gl_c[..., None, None] + jnp.einsum("bhcd,bhce->bhde", k_c, v_new)
        return state, o_c

    xs = tuple(jnp.moveaxis(t, 1, 0) for t in (q_dec, k_dec, u, w, qk, g_last))
    state0 = jnp.zeros((bsz, heads, dk, dv), dtype=jnp.float32)
    _, o = lax.scan(step, state0, xs)
    return o.transpose(1, 0, 3, 2, 4).reshape(bsz, seq, heads, dv)


def hybrid_mixer(xn, w_in, conv_qkv_w, a_log, dt_bias, gdn_norm_w, conv_sc_w, w_out):
    bsz, seq, _ = xn.shape
    proj = jnp.einsum("bsd,dc->bsc", xn, w_in)
    sizes = [3 * GDN_WIDTH, GDN_WIDTH, GDN_HEADS, GDN_HEADS, SC_WIDTH, SC_WIDTH, SC_WIDTH]
    idx = [sum(sizes[: i + 1]) for i in range(len(sizes) - 1)]
    qkv, z, a, b, sc_b, sc_c, sc_h = jnp.split(proj, idx, axis=-1)

    qkv = jax.nn.silu(causal_depthwise_conv(qkv, conv_qkv_w))
    q, k, v = (t.reshape(bsz, seq, GDN_HEADS, HEAD_DIM) for t in jnp.split(qkv, 3, axis=-1))
    g = -jnp.exp(a_log.astype(jnp.float32)) * jax.nn.softplus(a.astype(jnp.float32) + dt_bias.astype(jnp.float32))
    beta = jax.nn.sigmoid(b.astype(jnp.float32))
    o = gated_delta_rule_chunked(q, k, v, g, beta)
    o = o * lax.rsqrt(jnp.mean(o * o, axis=-1, keepdims=True) + NORM_EPS) * gdn_norm_w.astype(jnp.float32)
    o = o * jax.nn.silu(z.astype(jnp.float32).reshape(bsz, seq, GDN_HEADS, HEAD_DIM))
    gdn_out = o.reshape(bsz, seq, GDN_WIDTH).astype(xn.dtype)

    sc_out = sc_b * causal_depthwise_conv(sc_c * sc_h, conv_sc_w)

    mixed = jnp.concatenate([gdn_out, sc_out], axis=-1)
    return jnp.einsum("bsm,md->bsd", mixed, w_out)


def squared_relu_mlp(xn, w_up, w_down):
    hid = jnp.square(jax.nn.relu(jnp.einsum("bsd,df->bsf", xn, w_up)))
    return jnp.einsum("bsf,fd->bsd", hid, w_down)


def setup_inputs(seed: int = 0) -> dict:
    key = jax.random.key(seed)
    ks = jax.random.split(key, 16)
    f32 = jnp.float32

    def gain(k):
        return 1.0 + 0.02 * jax.random.normal(k, (DEPTH, D_MODEL), f32)

    x = jax.random.normal(ks[0], (BATCH, SEQ, D_MODEL), f32)
    norm_mix_pre = gain(ks[1])
    w_in = jax.random.normal(ks[2], (DEPTH, D_MODEL, IN_COLS), f32) * D_MODEL ** -0.5
    conv_qkv_w = jax.random.normal(ks[3], (DEPTH, GDN_CONV, 3 * GDN_WIDTH), f32) * GDN_CONV ** -0.5
    a_log = jnp.log(jax.random.uniform(ks[4], (DEPTH, GDN_HEADS), f32, 1.0, 16.0))
    dt_bias = 0.1 * jax.random.normal(ks[5], (DEPTH, GDN_HEADS), f32)
    gdn_norm_w = 1.0 + 0.02 * jax.random.normal(ks[6], (DEPTH, HEAD_DIM), f32)
    conv_sc_w = jax.random.normal(ks[7], (DEPTH, SC_CONV, SC_WIDTH), f32) * SC_CONV ** -0.5
    w_out = jax.random.normal(ks[8], (DEPTH, MIX_WIDTH, D_MODEL), f32) * MIX_WIDTH ** -0.5
    norm_mix_post = gain(ks[9])
    norm_mlp_pre = gain(ks[10])
    w_up = jax.random.normal(ks[11], (DEPTH, D_MODEL, D_FF), f32) * D_MODEL ** -0.5
    w_down = jax.random.normal(ks[12], (DEPTH, D_FF, D_MODEL), f32) * D_FF ** -0.5
    norm_mlp_post = gain(ks[13])
    return {"x": x, "norm_mix_pre": norm_mix_pre, "w_in": w_in, "conv_qkv_w": conv_qkv_w,
            "a_log": a_log, "dt_bias": dt_bias, "gdn_norm_w": gdn_norm_w, "conv_sc_w": conv_sc_w,
            "w_out": w_out, "norm_mix_post": norm_mix_post, "norm_mlp_pre": norm_mlp_pre,
            "w_up": w_up, "w_down": w_down, "norm_mlp_post": norm_mlp_post}


def reference(x, norm_mix_pre, w_in, conv_qkv_w, a_log, dt_bias, gdn_norm_w, conv_sc_w,
              w_out, norm_mix_post, norm_mlp_pre, w_up, w_down, norm_mlp_post):
    h = x
    for l in range(DEPTH):
        mix = hybrid_mixer(rms_norm(h, norm_mix_pre[l]), w_in[l], conv_qkv_w[l], a_log[l],
                           dt_bias[l], gdn_norm_w[l], conv_sc_w[l], w_out[l])
        h = h + rms_norm(mix, norm_mix_post[l])
        ff = squared_relu_mlp(rms_norm(h, norm_mlp_pre[l]), w_up[l], w_down[l])
        h = h + rms_norm(ff, norm_mlp_post[l])
    return h
```

```python
import functools

import jax
import jax.numpy as jnp
from jax import lax
from jax.experimental import pallas as pl
from jax.experimental.pallas import tpu as pltpu

F32 = jnp.float32
BF16 = jnp.bfloat16

D_MODEL = 4096
CHUNK = 64
HEAD_DIM = 128
GDN_WIDTH = 2048
GDN_HEADS = GDN_WIDTH // HEAD_DIM
SC_WIDTH = 2048
GDN_CONV = 4
SC_CONV = 3
NORM_EPS = 1e-6
L2_EPS = 1e-6

V7X_VMEM_LIMIT_BYTES = 56 * 1024 * 1024


def _params(*sem):
    return pltpu.CompilerParams(dimension_semantics=sem, vmem_limit_bytes=V7X_VMEM_LIMIT_BYTES)


def _sigmoid(x):
    return 1.0 / (1.0 + jnp.exp(-x))


def _rmsnorm_kernel(x_ref, g_ref, o_ref):
    x = x_ref[...]
    ms = jnp.mean(x * x, axis=-1, keepdims=True)
    o_ref[...] = (x * lax.rsqrt(ms + NORM_EPS) * g_ref[...]).astype(o_ref.dtype)


def _rmsnorm(x, g, tm=256):
    m, d = x.shape
    return pl.pallas_call(
        _rmsnorm_kernel,
        grid=(m // tm,),
        in_specs=[pl.BlockSpec((tm, d), lambda i: (i, 0)), pl.BlockSpec((1, d), lambda i: (0, 0))],
        out_specs=pl.BlockSpec((tm, d), lambda i: (i, 0)),
        out_shape=jax.ShapeDtypeStruct((m, d), BF16),
        compiler_params=_params("parallel"),
        name="rmsnorm",
    )(x, g.reshape(1, d))


def _mm_kernel(a_ref, w_ref, o_ref, *scratch, nk, act):
    part = jnp.dot(a_ref[...], w_ref[...], preferred_element_type=F32)

    def finish(acc):
        if act == "relu2":
            r = jnp.maximum(acc, 0.0)
            acc = r * r
        o_ref[...] = acc.astype(o_ref.dtype)

    if nk == 1:
        finish(part)
        return
    (acc_ref,) = scratch
    k = pl.program_id(2)

    @pl.when(k == 0)
    def _():
        acc_ref[...] = part

    @pl.when(jnp.logical_and(k > 0, k < nk - 1))
    def _():
        acc_ref[...] += part

    @pl.when(k == nk - 1)
    def _():
        finish(acc_ref[...] + part)


def _matmul(a, w, *, tm, tn, tk, out_dtype, act=None, name="matmul"):
    m, kdim = a.shape
    _, n = w.shape
    nk = kdim // tk
    scratch = [pltpu.VMEM((tm, tn), F32)] if nk > 1 else []
    return pl.pallas_call(
        functools.partial(_mm_kernel, nk=nk, act=act),
        grid=(m // tm, n // tn, nk),
        in_specs=[pl.BlockSpec((tm, tk), lambda i, j, k: (i, k)), pl.BlockSpec((tk, tn), lambda i, j, k: (k, j))],
        out_specs=pl.BlockSpec((tm, tn), lambda i, j, k: (i, j)),
        out_shape=jax.ShapeDtypeStruct((m, n), out_dtype),
        scratch_shapes=scratch,
        compiler_params=_params("parallel", "parallel", "arbitrary"),
        name=name,
    )(a, w)


def _gate_kernel(a_ref, w_ref, alog_ref, dtb_ref, g_ref, beta_ref):
    p = jnp.dot(a_ref[...], w_ref[...], preferred_element_type=F32)
    a = p[:, :128] + dtb_ref[...]
    softplus = jnp.maximum(a, 0.0) + jnp.log(1.0 + jnp.exp(-jnp.abs(a)))
    g_ref[...] = -jnp.exp(alog_ref[...]) * softplus
    beta_ref[...] = _sigmoid(p[:, 128:])


def _gate_proj(xn, w_ab, a_log, dt_bias, tm=1024):
    m, d = xn.shape
    row = lambda i: (i, 0)
    fixed = lambda i: (0, 0)
    return pl.pallas_call(
        _gate_kernel,
        grid=(m // tm,),
        in_specs=[pl.BlockSpec((tm, d), row), pl.BlockSpec((d, 256), fixed),
                  pl.BlockSpec((1, 128), fixed), pl.BlockSpec((1, 128), fixed)],
        out_specs=[pl.BlockSpec((tm, 128), row), pl.BlockSpec((tm, 128), row)],
        out_shape=[jax.ShapeDtypeStruct((m, 128), F32), jax.ShapeDtypeStruct((m, 128), F32)],
        compiler_params=_params("parallel"),
        name="gate_proj",
    )(xn, w_ab, a_log, dt_bias)


def _gdn_kernel(q_ref, k_ref, v_ref, z_ref, g_ref, b_ref, wq_ref, wk_ref, wv_ref, nw_ref, o_ref, s_ref, *, hb, nchunk):
    head0 = pl.program_id(1) * hb
    s_ref[...] = jnp.zeros_like(s_ref)
    row = lax.broadcasted_iota(jnp.int32, (CHUNK, CHUNK), 0)
    col = lax.broadcasted_iota(jnp.int32, (CHUNK, CHUNK), 1)
    tri_incl = row >= col
    tri_strict = row > col
    tri_f = tri_incl.astype(F32)
    eye = (row == col).astype(F32)
    shift = (128 - head0) % 128
    nt = (((1,), (1,)), ((), ()))
    tn = (((0,), (0,)), ((), ()))

    def chunk(c, carry):
        r0 = pl.multiple_of(c * CHUNK, CHUNK)
        rp = pl.multiple_of(jnp.maximum(r0 - 8, 0), 8)
        keep = (c > 0).astype(F32)
        rows = pl.ds(r0, CHUNK)

        graw = pltpu.roll(g_ref[0, rows, :], shift, axis=1)
        beta = pltpu.roll(b_ref[0, rows, :], shift, axis=1)
        gc = jnp.dot(tri_f, graw, preferred_element_type=F32, precision=lax.Precision.HIGHEST)
        eg = jnp.exp(gc)
        g_last = gc[CHUNK - 1:CHUNK, :]
        eg_rest = jnp.exp(g_last - gc)
        eg_last = jnp.exp(g_last)
        gc_t = gc.T

        def conv_silu(ref, w_ref, lanes):
            cur = ref[0, rows, lanes]
            prev = ref[0, pl.ds(rp, 8), lanes] * keep
            ext = jnp.concatenate([prev, cur], axis=0)
            w = w_ref[:, lanes]
            acc = cur * w[GDN_CONV - 1:GDN_CONV, :]
            for t in range(GDN_CONV - 1):
                off = 8 - (GDN_CONV - 1) + t
                acc = acc + ext[off:off + CHUNK, :] * w[t:t + 1, :]
            return acc * _sigmoid(acc)

        for j in range(hb):
            lanes = slice(j * HEAD_DIM, (j + 1) * HEAD_DIM)
            q = conv_silu(q_ref, wq_ref, lanes)
            k = conv_silu(k_ref, wk_ref, lanes)
            v = conv_silu(v_ref, wv_ref, lanes)
            q = q * (lax.rsqrt(jnp.sum(q * q, axis=-1, keepdims=True) + L2_EPS) * (HEAD_DIM ** -0.5))
            k = k * lax.rsqrt(jnp.sum(k * k, axis=-1, keepdims=True) + L2_EPS)
            qb = q.astype(BF16)
            kb = k.astype(BF16)
            kk = lax.dot_general(kb, kb, nt, preferred_element_type=F32)
            qk = lax.dot_general(qb, kb, nt, preferred_element_type=F32)

            g_col = gc[:, j:j + 1]
            beta_col = beta[:, j:j + 1]
            decay = jnp.where(tri_incl, jnp.exp(g_col - gc_t[j:j + 1, :]), 0.0)
            low = jnp.where(tri_strict, beta_col * kk * decay, 0.0)

            inv = eye - low
            pw = low
            for _ in range(5):
                pwb = pw.astype(BF16)
                pw = jnp.dot(pwb, pwb, preferred_element_type=F32)
                inv = inv + jnp.dot(inv.astype(BF16), pw.astype(BF16), preferred_element_type=F32)

            rhs = jnp.concatenate([v * beta_col, k * (beta_col * eg[:, j:j + 1])], axis=1)
            sol = jnp.dot(inv.astype(BF16), rhs.astype(BF16), preferred_element_type=F32)
            u = sol[:, :HEAD_DIM]
            w = sol[:, HEAD_DIM:]
            q_dec = q * eg[:, j:j + 1]
            k_dec = k * eg_rest[:, j:j + 1]

            state = s_ref[j]
            wq = jnp.concatenate([w, q_dec], axis=0).astype(BF16)
            proj = jnp.dot(wq, state.astype(BF16), preferred_element_type=F32)
            v_new = u - proj[:CHUNK]
            v_new_b = v_new.astype(BF16)
            o = proj[CHUNK:] + jnp.dot((qk * decay).astype(BF16), v_new_b, preferred_element_type=F32)
            s_ref[j] = state * eg_last[:, j:j + 1] + lax.dot_general(
                k_dec.astype(BF16), v_new_b, tn, preferred_element_type=F32)

            ms = jnp.mean(o * o, axis=-1, keepdims=True)
            z = z_ref[0, rows, lanes]
            o = o * lax.rsqrt(ms + NORM_EPS) * nw_ref[...] * (z * _sigmoid(z))
            o_ref[0, rows, lanes] = o.astype(o_ref.dtype)
        return carry

    lax.fori_loop(0, nchunk, chunk, 0)


def _gdn(proj, g, beta, conv_w, norm_w, *, bsz, seq, hb=2):
    nblk = GDN_WIDTH // (HEAD_DIM * hb)
    wblk = HEAD_DIM * hb

    def col(group):
        return pl.BlockSpec((1, seq, wblk), lambda b, h: (b, 0, group * nblk + h))

    def wcol(group):
        return pl.BlockSpec((GDN_CONV, wblk), lambda b, h: (0, group * nblk + h))

    gspec = pl.BlockSpec((1, seq, 128), lambda b, h: (b, 0, 0))
    return pl.pallas_call(
        functools.partial(_gdn_kernel, hb=hb, nchunk=seq // CHUNK),
        grid=(bsz, nblk),
        in_specs=[col(0), col(1), col(2), col(3), gspec, gspec, wcol(0), wcol(1), wcol(2),
                  pl.BlockSpec((1, HEAD_DIM), lambda b, h: (0, 0))],
        out_specs=pl.BlockSpec((1, seq, wblk), lambda b, h: (b, 0, h)),
        out_shape=jax.ShapeDtypeStruct((bsz, seq, GDN_WIDTH), BF16),
        scratch_shapes=[pltpu.VMEM((hb, HEAD_DIM, HEAD_DIM), F32)],
        compiler_params=_params("parallel", "parallel"),
        name="gdn",
    )(proj, proj, proj, proj, g, beta, conv_w, conv_w, conv_w, norm_w)


def _sconv_kernel(b_ref, c_ref, h_ref, w_ref, o_ref, *, rows_per_step, nstep):
    w = w_ref[...]

    def step(i, carry):
        r0 = pl.multiple_of(i * rows_per_step, rows_per_step)
        rp = pl.multiple_of(jnp.maximum(r0 - 8, 0), 8)
        keep = (i > 0).astype(F32)
        rows = pl.ds(r0, rows_per_step)
        cur = c_ref[0, rows, :] * h_ref[0, rows, :]
        prev = c_ref[0, pl.ds(rp, 8), :] * h_ref[0, pl.ds(rp, 8), :] * keep
        ext = jnp.concatenate([prev, cur], axis=0)
        acc = cur * w[SC_CONV - 1:SC_CONV, :]
        for t in range(SC_CONV - 1):
            off = 8 - (SC_CONV - 1) + t
            acc = acc + ext[off:off + rows_per_step, :] * w[t:t + 1, :]
        o_ref[0, rows, :] = (b_ref[0, rows, :] * acc).astype(o_ref.dtype)
        return carry

    lax.fori_loop(0, nstep, step, 0)


def _sconv(proj, conv_w, *, bsz, seq, col0, tc=256, rows_per_step=64):
    nblk = SC_WIDTH // tc
    base = col0 // tc

    def col(group):
        return pl.BlockSpec((1, seq, tc), lambda b, j: (b, 0, base + group * nblk + j))

    return pl.pallas_call(
        functools.partial(_sconv_kernel, rows_per_step=rows_per_step, nstep=seq // rows_per_step),
        grid=(bsz, nblk),
        in_specs=[col(0), col(1), col(2), pl.BlockSpec((SC_CONV, tc), lambda b, j: (0, j))],
        out_specs=pl.BlockSpec((1, seq, tc), lambda b, j: (b, 0, j)),
        out_shape=jax.ShapeDtypeStruct((bsz, seq, SC_WIDTH), BF16),
        compiler_params=_params("parallel", "parallel"),
        name="sconv",
    )(proj, proj, proj, conv_w)


def _res_norm_kernel(t_ref, r_ref, gpost_ref, gnext_ref, h_ref, hn_ref):
    t = t_ref[...]
    h = r_ref[...] + t * lax.rsqrt(jnp.mean(t * t, axis=-1, keepdims=True) + NORM_EPS) * gpost_ref[...]
    h_ref[...] = h
    hn = h * lax.rsqrt(jnp.mean(h * h, axis=-1, keepdims=True) + NORM_EPS) * gnext_ref[...]
    hn_ref[...] = hn.astype(hn_ref.dtype)


def _res_norm(t, resid, g_post, g_next, tm=256):
    m, d = t.shape
    row = lambda i: (i, 0)
    fixed = lambda i: (0, 0)
    return pl.pallas_call(
        _res_norm_kernel,
        grid=(m // tm,),
        in_specs=[pl.BlockSpec((tm, d), row), pl.BlockSpec((tm, d), row),
                  pl.BlockSpec((1, d), fixed), pl.BlockSpec((1, d), fixed)],
        out_specs=[pl.BlockSpec((tm, d), row), pl.BlockSpec((tm, d), row)],
        out_shape=[jax.ShapeDtypeStruct((m, d), F32), jax.ShapeDtypeStruct((m, d), BF16)],
        compiler_params=_params("parallel"),
        name="res_norm",
    )(t, resid, g_post.reshape(1, d), g_next.reshape(1, d))


def _res_final_kernel(t_ref, r_ref, g_ref, o_ref):
    t = t_ref[...]
    o_ref[...] = r_ref[...] + t * lax.rsqrt(jnp.mean(t * t, axis=-1, keepdims=True) + NORM_EPS) * g_ref[...]


def _res_final(t, resid, g, tm=256):
    m, d = t.shape
    row = lambda i: (i, 0)
    return pl.pallas_call(
        _res_final_kernel,
        grid=(m // tm,),
        in_specs=[pl.BlockSpec((tm, d), row), pl.BlockSpec((tm, d), row), pl.BlockSpec((1, d), lambda i: (0, 0))],
        out_specs=pl.BlockSpec((tm, d), row),
        out_shape=jax.ShapeDtypeStruct((m, d), F32),
        compiler_params=_params("parallel"),
        name="res_final",
    )(t, resid, g.reshape(1, d))


def _layer(h, norm_mix_pre, w_in, conv_qkv_w, a_log, dt_bias, gdn_norm_w, conv_sc_w, w_out,
           norm_mix_post, norm_mlp_pre, w_up, w_down, norm_mlp_post, *, bsz, seq):
    m = bsz * seq
    n_gdn = 4 * GDN_WIDTH
    n_ab = 2 * GDN_HEADS

    xn = _rmsnorm(h, norm_mix_pre)

    w_main = jnp.concatenate([w_in[:, :n_gdn], w_in[:, n_gdn + n_ab:]], axis=1).astype(BF16)
    pad = jnp.zeros((D_MODEL, 128 - GDN_HEADS), F32)
    w_ab = jnp.concatenate([w_in[:, n_gdn:n_gdn + GDN_HEADS], pad,
                            w_in[:, n_gdn + GDN_HEADS:n_gdn + n_ab], pad], axis=1).astype(BF16)
    lane_pad = jnp.zeros((1, 128 - GDN_HEADS), F32)
    a_log_p = jnp.concatenate([a_log.reshape(1, -1), lane_pad], axis=1)
    dt_bias_p = jnp.concatenate([dt_bias.reshape(1, -1), lane_pad], axis=1)

    proj = _matmul(xn, w_main, tm=1024, tn=1024, tk=D_MODEL, out_dtype=F32, name="in_proj")
    g, beta = _gate_proj(xn, w_ab, a_log_p, dt_bias_p)

    proj3 = proj.reshape(bsz, seq, -1)
    gdn_out = _gdn(proj3, g.reshape(bsz, seq, 128), beta.reshape(bsz, seq, 128), conv_qkv_w,
                   gdn_norm_w.reshape(1, HEAD_DIM), bsz=bsz, seq=seq)
    sc_out = _sconv(proj3, conv_sc_w, bsz=bsz, seq=seq, col0=n_gdn)

    mixed = jnp.concatenate([gdn_out, sc_out], axis=-1).reshape(m, -1)
    mix = _matmul(mixed, w_out.astype(BF16), tm=1024, tn=1024, tk=D_MODEL, out_dtype=F32, name="out_proj")
    h, hn = _res_norm(mix, h, norm_mix_post, norm_mlp_pre)

    hid = _matmul(hn, w_up.astype(BF16), tm=1024, tn=1024, tk=D_MODEL, out_dtype=BF16, act="relu2", name="mlp_up")
    ff = _matmul(hid, w_down.astype(BF16), tm=1024, tn=1024, tk=D_MODEL, out_dtype=F32, name="mlp_down")
    return _res_final(ff, h, norm_mlp_post)


def kernel(x, norm_mix_pre, w_in, conv_qkv_w, a_log, dt_bias, gdn_norm_w, conv_sc_w, w_out, norm_mix_post,
           norm_mlp_pre, w_up, w_down, norm_mlp_post):
    bsz, seq, d = x.shape
    h = x.reshape(bsz * seq, d)
    for l in range(norm_mix_pre.shape[0]):
        h = _layer(h, norm_mix_pre[l], w_in[l], conv_qkv_w[l], a_log[l], dt_bias[l], gdn_norm_w[l], conv_sc_w[l],
                   w_out[l], norm_mix_post[l], norm_mlp_pre[l], w_up[l], w_down[l], norm_mlp_post[l],
                   bsz=bsz, seq=seq)
    return h.reshape(bsz, seq, d)
```

```python
import functools

import jax
import jax.numpy as jnp
from jax import lax
from jax.experimental import pallas as pl
from jax.experimental.pallas import tpu as pltpu

F32 = jnp.float32
BF16 = jnp.bfloat16

D_MODEL = 4096
CHUNK = 64
HEAD_DIM = 128
GDN_WIDTH = 2048
GDN_HEADS = GDN_WIDTH // HEAD_DIM
SC_WIDTH = 2048
GDN_CONV = 4
SC_CONV = 3
NORM_EPS = 1e-6
L2_EPS = 1e-6

V7X_VMEM_LIMIT_BYTES = 56 * 1024 * 1024


def _params(*sem):
    return pltpu.CompilerParams(dimension_semantics=sem, vmem_limit_bytes=V7X_VMEM_LIMIT_BYTES)


def _sigmoid(x):
    return 1.0 / (1.0 + jnp.exp(-x))


def _rmsnorm_kernel(x_ref, g_ref, o_ref):
    x = x_ref[...]
    ms = jnp.mean(x * x, axis=-1, keepdims=True)
    o_ref[...] = (x * lax.rsqrt(ms + NORM_EPS) * g_ref[...]).astype(o_ref.dtype)


def _rmsnorm(x, g, tm=256):
    m, d = x.shape
    return pl.pallas_call(
        _rmsnorm_kernel,
        grid=(m // tm,),
        in_specs=[pl.BlockSpec((tm, d), lambda i: (i, 0)), pl.BlockSpec((1, d), lambda i: (0, 0))],
        out_specs=pl.BlockSpec((tm, d), lambda i: (i, 0)),
        out_shape=jax.ShapeDtypeStruct((m, d), BF16),
        compiler_params=_params("parallel"),
        name="rmsnorm",
    )(x, g.reshape(1, d))


def _mm_kernel(a_ref, w_ref, o_ref, *scratch, nk, act):
    part = jnp.dot(a_ref[...], w_ref[...], preferred_element_type=F32)

    def finish(acc):
        if act == "relu2":
            r = jnp.maximum(acc, 0.0)
            acc = r * r
        o_ref[...] = acc.astype(o_ref.dtype)

    if nk == 1:
        finish(part)
        return
    (acc_ref,) = scratch
    k = pl.program_id(2)

    @pl.when(k == 0)
    def _():
        acc_ref[...] = part

    @pl.when(jnp.logical_and(k > 0, k < nk - 1))
    def _():
        acc_ref[...] += part

    @pl.when(k == nk - 1)
    def _():
        finish(acc_ref[...] + part)


def _matmul(a, w, *, tm, tn, tk, out_dtype, act=None, name="matmul"):
    m, kdim = a.shape
    _, n = w.shape
    nk = kdim // tk
    scratch = [pltpu.VMEM((tm, tn), F32)] if nk > 1 else []
    return pl.pallas_call(
        functools.partial(_mm_kernel, nk=nk, act=act),
        grid=(m // tm, n // tn, nk),
        in_specs=[pl.BlockSpec((tm, tk), lambda i, j, k: (i, k)), pl.BlockSpec((tk, tn), lambda i, j, k: (k, j))],
        out_specs=pl.BlockSpec((tm, tn), lambda i, j, k: (i, j)),
        out_shape=jax.ShapeDtypeStruct((m, n), out_dtype),
        scratch_shapes=scratch,
        compiler_params=_params("parallel", "parallel", "arbitrary"),
        name=name,
    )(a, w)


def _mm_pair_kernel(a1_ref, a2_ref, w1_ref, w2_ref, o_ref):
    acc = jnp.dot(a1_ref[...], w1_ref[...], preferred_element_type=F32)
    acc = acc + jnp.dot(a2_ref[...], w2_ref[...], preferred_element_type=F32)
    o_ref[...] = acc.astype(o_ref.dtype)


def _matmul_pair(a1, a2, w, *, tm, tn, out_dtype, name):
    m, k1 = a1.shape
    _, k2 = a2.shape
    assert k1 == k2 and w.shape[0] == k1 + k2
    n = w.shape[1]
    return pl.pallas_call(
        _mm_pair_kernel,
        grid=(m // tm, n // tn),
        in_specs=[pl.BlockSpec((tm, k1), lambda i, j: (i, 0)), pl.BlockSpec((tm, k2), lambda i, j: (i, 0)),
                  pl.BlockSpec((k1, tn), lambda i, j: (0, j)), pl.BlockSpec((k2, tn), lambda i, j: (1, j))],
        out_specs=pl.BlockSpec((tm, tn), lambda i, j: (i, j)),
        out_shape=jax.ShapeDtypeStruct((m, n), out_dtype),
        compiler_params=_params("parallel", "parallel"),
        name=name,
    )(a1, a2, w, w)


def _gate_kernel(a_ref, w_ref, alog_ref, dtb_ref, g_ref, beta_ref):
    p = jnp.dot(a_ref[...], w_ref[...], preferred_element_type=F32)
    a = p[:, :128] + dtb_ref[...]
    softplus = jnp.maximum(a, 0.0) + jnp.log(1.0 + jnp.exp(-jnp.abs(a)))
    g_ref[...] = -jnp.exp(alog_ref[...]) * softplus
    beta_ref[...] = _sigmoid(p[:, 128:])


def _gate_proj(xn, w_ab, a_log, dt_bias, tm=1024):
    m, d = xn.shape
    row = lambda i: (i, 0)
    fixed = lambda i: (0, 0)
    return pl.pallas_call(
        _gate_kernel,
        grid=(m // tm,),
        in_specs=[pl.BlockSpec((tm, d), row), pl.BlockSpec((d, 256), fixed),
                  pl.BlockSpec((1, 128), fixed), pl.BlockSpec((1, 128), fixed)],
        out_specs=[pl.BlockSpec((tm, 128), row), pl.BlockSpec((tm, 128), row)],
        out_shape=[jax.ShapeDtypeStruct((m, 128), F32), jax.ShapeDtypeStruct((m, 128), F32)],
        compiler_params=_params("parallel"),
        name="gate_proj",
    )(xn, w_ab, a_log, dt_bias)


def _gdn_kernel(q_ref, k_ref, v_ref, z_ref, qh_ref, kh_ref, vh_ref, g_ref, b_ref, wq_ref, wk_ref, wv_ref, nw_ref,
                o_ref, s_ref, u_ref, wqd_ref, kdt_ref, qkm_ref, egl_ref, *, hb, nchunk):
    head0 = pl.program_id(1) * hb
    tile = pl.program_id(2)

    @pl.when(tile == 0)
    def _():
        s_ref[...] = jnp.zeros_like(s_ref)

    row = lax.broadcasted_iota(jnp.int32, (CHUNK, CHUNK), 0)
    col = lax.broadcasted_iota(jnp.int32, (CHUNK, CHUNK), 1)
    tri_incl = row >= col
    tri_strict = row > col
    tri_f = tri_incl.astype(F32)
    eye = (row == col).astype(F32)
    shift = (128 - head0) % 128
    nt = (((1,), (1,)), ((), ()))
    keep_halo = (tile > 0).astype(F32)

    def precompute(c, carry):
        r0 = pl.multiple_of(c * CHUNK, CHUNK)
        rp = pl.multiple_of(jnp.maximum(r0 - 8, 0), 8)
        first = c == 0
        rows = pl.ds(r0, CHUNK)

        graw = pltpu.roll(g_ref[0, rows, :], shift, axis=1)
        beta = pltpu.roll(b_ref[0, rows, :], shift, axis=1)
        gc = jnp.dot(tri_f, graw, preferred_element_type=F32, precision=lax.Precision.HIGHEST)
        eg = jnp.exp(gc)
        g_last = gc[CHUNK - 1:CHUNK, :]
        eg_rest = jnp.exp(g_last - gc)
        egl_ref[c] = jnp.broadcast_to(jnp.exp(g_last), (8, 128))
        gc_t = gc.T

        def conv_silu(ref, halo_ref, w_ref, lanes):
            cur = ref[0, rows, lanes]
            prev = jnp.where(first, halo_ref[0, :, lanes] * keep_halo, ref[0, pl.ds(rp, 8), lanes])
            ext = jnp.concatenate([prev, cur], axis=0)
            w = w_ref[:, lanes]
            acc = cur * w[GDN_CONV - 1:GDN_CONV, :]
            for t in range(GDN_CONV - 1):
                off = 8 - (GDN_CONV - 1) + t
                acc = acc + ext[off:off + CHUNK, :] * w[t:t + 1, :]
            return acc * _sigmoid(acc)

        heads = range(hb)
        lanes = [slice(j * HEAD_DIM, (j + 1) * HEAD_DIM) for j in heads]
        q = [conv_silu(q_ref, qh_ref, wq_ref, lanes[j]) for j in heads]
        k = [conv_silu(k_ref, kh_ref, wk_ref, lanes[j]) for j in heads]
        v = [conv_silu(v_ref, vh_ref, wv_ref, lanes[j]) for j in heads]
        q = [x * (lax.rsqrt(jnp.sum(x * x, axis=-1, keepdims=True) + L2_EPS) * (HEAD_DIM ** -0.5)) for x in q]
        k = [x * lax.rsqrt(jnp.sum(x * x, axis=-1, keepdims=True) + L2_EPS) for x in k]
        kb = [x.astype(BF16) for x in k]
        kk = [lax.dot_general(kb[j], kb[j], nt, preferred_element_type=F32) for j in heads]
        qk = [lax.dot_general(q[j].astype(BF16), kb[j], nt, preferred_element_type=F32) for j in heads]

        beta_col = [beta[:, j:j + 1] for j in heads]
        decay = [jnp.where(tri_incl, jnp.exp(gc[:, j:j + 1] - gc_t[j:j + 1, :]), 0.0) for j in heads]
        low = [jnp.where(tri_strict, beta_col[j] * kk[j] * decay[j], 0.0) for j in heads]

        inv = [eye - x for x in low]
        pw = low
        for _ in range(5):
            pwb = [x.astype(BF16) for x in pw]
            pw = [jnp.dot(x, x, preferred_element_type=F32) for x in pwb]
            inv = [inv[j] + jnp.dot(inv[j].astype(BF16), pw[j].astype(BF16), preferred_element_type=F32)
                   for j in heads]

        rhs = [jnp.concatenate([v[j] * beta_col[j], k[j] * (beta_col[j] * eg[:, j:j + 1])], axis=1) for j in heads]
        sol = [jnp.dot(inv[j].astype(BF16), rhs[j].astype(BF16), preferred_element_type=F32) for j in heads]
        for j in heads:
            u_ref[j, rows, :] = sol[j][:, :HEAD_DIM]
            wqd_ref[j, c, :CHUNK, :] = sol[j][:, HEAD_DIM:].astype(BF16)
            wqd_ref[j, c, CHUNK:, :] = (q[j] * eg[:, j:j + 1]).astype(BF16)
            kdt_ref[j, c] = (k[j] * eg_rest[:, j:j + 1]).T.astype(BF16)
            qkm_ref[j, rows, :] = (qk[j] * decay[j]).astype(BF16)
        return carry

    def scan(c, carry):
        r0 = pl.multiple_of(c * CHUNK, CHUNK)
        rows = pl.ds(r0, CHUNK)
        egl = egl_ref[c]
        heads = range(hb)
        state = [s_ref[j] for j in heads]
        proj = [jnp.dot(wqd_ref[j, c], state[j].astype(BF16), preferred_element_type=F32) for j in heads]
        v_new = [(u_ref[j, rows, :] - proj[j][:CHUNK]).astype(BF16) for j in heads]
        for j in heads:
            s_ref[j] = state[j] * egl[0:1, j:j + 1] + jnp.dot(kdt_ref[j, c], v_new[j], preferred_element_type=F32)
        o = [proj[j][CHUNK:] + jnp.dot(qkm_ref[j, rows, :], v_new[j], preferred_element_type=F32) for j in heads]
        for j in heads:
            lanes = slice(j * HEAD_DIM, (j + 1) * HEAD_DIM)
            ms = jnp.mean(o[j] * o[j], axis=-1, keepdims=True)
            z = z_ref[0, rows, lanes]
            out = o[j] * lax.rsqrt(ms + NORM_EPS) * nw_ref[...] * (z * _sigmoid(z))
            o_ref[0, rows, lanes] = out.astype(o_ref.dtype)
        return carry

    lax.fori_loop(0, nchunk, precompute, 0)
    lax.fori_loop(0, nchunk, scan, 0)


def _gdn(proj, g, beta, conv_w, norm_w, *, bsz, seq, hb=8, ts=512):
    nblk = GDN_WIDTH // (HEAD_DIM * hb)
    wblk = HEAD_DIM * hb
    nchunk = ts // CHUNK

    def col(group):
        return pl.BlockSpec((1, ts, wblk), lambda b, h, t: (b, t, group * nblk + h))

    def halo(group):
        return pl.BlockSpec((1, 8, wblk), lambda b, h, t: (b, jnp.maximum(t * (ts // 8) - 1, 0), group * nblk + h))

    def wcol(group):
        return pl.BlockSpec((GDN_CONV, wblk), lambda b, h, t: (0, group * nblk + h))

    gspec = pl.BlockSpec((1, ts, 128), lambda b, h, t: (b, t, 0))
    return pl.pallas_call(
        functools.partial(_gdn_kernel, hb=hb, nchunk=nchunk),
        grid=(bsz, nblk, seq // ts),
        in_specs=[col(0), col(1), col(2), col(3), halo(0), halo(1), halo(2), gspec, gspec,
                  wcol(0), wcol(1), wcol(2), pl.BlockSpec((1, HEAD_DIM), lambda b, h, t: (0, 0))],
        out_specs=pl.BlockSpec((1, ts, wblk), lambda b, h, t: (b, t, h)),
        out_shape=jax.ShapeDtypeStruct((bsz, seq, GDN_WIDTH), BF16),
        scratch_shapes=[pltpu.VMEM((hb, HEAD_DIM, HEAD_DIM), F32),
                        pltpu.VMEM((hb, ts, HEAD_DIM), F32),
                        pltpu.VMEM((hb, nchunk, 2 * CHUNK, HEAD_DIM), BF16),
                        pltpu.VMEM((hb, nchunk, HEAD_DIM, CHUNK), BF16),
                        pltpu.VMEM((hb, ts, CHUNK), BF16),
                        pltpu.VMEM((nchunk, 8, 128), F32)],
        compiler_params=_params("parallel", "parallel", "arbitrary"),
        name="gdn",
    )(proj, proj, proj, proj, proj, proj, proj, g, beta, conv_w, conv_w, conv_w, norm_w)


def _sconv_kernel(b_ref, c_ref, h_ref, w_ref, o_ref, *, rows_per_step, nstep):
    w = w_ref[...]

    def step(i, carry):
        r0 = pl.multiple_of(i * rows_per_step, rows_per_step)
        rp = pl.multiple_of(jnp.maximum(r0 - 8, 0), 8)
        keep = (i > 0).astype(F32)
        rows = pl.ds(r0, rows_per_step)
        cur = c_ref[0, rows, :] * h_ref[0, rows, :]
        prev = c_ref[0, pl.ds(rp, 8), :] * h_ref[0, pl.ds(rp, 8), :] * keep
        ext = jnp.concatenate([prev, cur], axis=0)
        acc = cur * w[SC_CONV - 1:SC_CONV, :]
        for t in range(SC_CONV - 1):
            off = 8 - (SC_CONV - 1) + t
            acc = acc + ext[off:off + rows_per_step, :] * w[t:t + 1, :]
        o_ref[0, rows, :] = (b_ref[0, rows, :] * acc).astype(o_ref.dtype)
        return carry

    lax.fori_loop(0, nstep, step, 0)


def _sconv(proj, conv_w, *, bsz, seq, col0, tc=256, rows_per_step=64):
    nblk = SC_WIDTH // tc
    base = col0 // tc

    def col(group):
        return pl.BlockSpec((1, seq, tc), lambda b, j: (b, 0, base + group * nblk + j))

    return pl.pallas_call(
        functools.partial(_sconv_kernel, rows_per_step=rows_per_step, nstep=seq // rows_per_step),
        grid=(bsz, nblk),
        in_specs=[col(0), col(1), col(2), pl.BlockSpec((SC_CONV, tc), lambda b, j: (0, j))],
        out_specs=pl.BlockSpec((1, seq, tc), lambda b, j: (b, 0, j)),
        out_shape=jax.ShapeDtypeStruct((bsz, seq, SC_WIDTH), BF16),
        compiler_params=_params("parallel", "parallel"),
        name="sconv",
    )(proj, proj, proj, conv_w)


def _res_norm_kernel(t_ref, r_ref, gpost_ref, gnext_ref, h_ref, hn_ref):
    t = t_ref[...]
    h = r_ref[...] + t * lax.rsqrt(jnp.mean(t * t, axis=-1, keepdims=True) + NORM_EPS) * gpost_ref[...]
    h_ref[...] = h
    hn = h * lax.rsqrt(jnp.mean(h * h, axis=-1, keepdims=True) + NORM_EPS) * gnext_ref[...]
    hn_ref[...] = hn.astype(hn_ref.dtype)


def _res_norm(t, resid, g_post, g_next, tm=256):
    m, d = t.shape
    row = lambda i: (i, 0)
    fixed = lambda i: (0, 0)
    return pl.pallas_call(
        _res_norm_kernel,
        grid=(m // tm,),
        in_specs=[pl.BlockSpec((tm, d), row), pl.BlockSpec((tm, d), row),
                  pl.BlockSpec((1, d), fixed), pl.BlockSpec((1, d), fixed)],
        out_specs=[pl.BlockSpec((tm, d), row), pl.BlockSpec((tm, d), row)],
        out_shape=[jax.ShapeDtypeStruct((m, d), F32), jax.ShapeDtypeStruct((m, d), BF16)],
        compiler_params=_params("parallel"),
        name="res_norm",
    )(t, resid, g_post.reshape(1, d), g_next.reshape(1, d))


def _res_final_kernel(t_ref, r_ref, g_ref, o_ref):
    t = t_ref[...]
    o_ref[...] = r_ref[...] + t * lax.rsqrt(jnp.mean(t * t, axis=-1, keepdims=True) + NORM_EPS) * g_ref[...]


def _res_final(t, resid, g, tm=256):
    m, d = t.shape
    row = lambda i: (i, 0)
    return pl.pallas_call(
        _res_final_kernel,
        grid=(m // tm,),
        in_specs=[pl.BlockSpec((tm, d), row), pl.BlockSpec((tm, d), row), pl.BlockSpec((1, d), lambda i: (0, 0))],
        out_specs=pl.BlockSpec((tm, d), row),
        out_shape=jax.ShapeDtypeStruct((m, d), F32),
        compiler_params=_params("parallel"),
        name="res_final",
    )(t, resid, g.reshape(1, d))


def _layer(h, norm_mix_pre, w_in, conv_qkv_w, a_log, dt_bias, gdn_norm_w, conv_sc_w, w_out,
           norm_mix_post, norm_mlp_pre, w_up, w_down, norm_mlp_post, *, bsz, seq):
    m = bsz * seq
    n_gdn = 4 * GDN_WIDTH
    n_ab = 2 * GDN_HEADS

    xn = _rmsnorm(h, norm_mix_pre)

    w_main = jnp.concatenate([w_in[:, :n_gdn], w_in[:, n_gdn + n_ab:]], axis=1).astype(BF16)
    pad = jnp.zeros((D_MODEL, 128 - GDN_HEADS), F32)
    w_ab = jnp.concatenate([w_in[:, n_gdn:n_gdn + GDN_HEADS], pad,
                            w_in[:, n_gdn + GDN_HEADS:n_gdn + n_ab], pad], axis=1).astype(BF16)
    lane_pad = jnp.zeros((1, 128 - GDN_HEADS), F32)
    a_log_p = jnp.concatenate([a_log.reshape(1, -1), lane_pad], axis=1)
    dt_bias_p = jnp.concatenate([dt_bias.reshape(1, -1), lane_pad], axis=1)

    proj = _matmul(xn, w_main, tm=1024, tn=1024, tk=D_MODEL, out_dtype=F32, name="in_proj")
    g, beta = _gate_proj(xn, w_ab, a_log_p, dt_bias_p)

    proj3 = proj.reshape(bsz, seq, -1)
    gdn_out = _gdn(proj3, g.reshape(bsz, seq, 128), beta.reshape(bsz, seq, 128), conv_qkv_w,
                   gdn_norm_w.reshape(1, HEAD_DIM), bsz=bsz, seq=seq)
    sc_out = _sconv(proj3, conv_sc_w, bsz=bsz, seq=seq, col0=n_gdn)

    mix = _matmul_pair(gdn_out.reshape(m, -1), sc_out.reshape(m, -1), w_out.astype(BF16), tm=1024, tn=1024,
                       out_dtype=F32, name="out_proj")
    h, hn = _res_norm(mix, h, norm_mix_post, norm_mlp_pre)

    hid = _matmul(hn, w_up.astype(BF16), tm=1024, tn=1024, tk=D_MODEL, out_dtype=BF16, act="relu2", name="mlp_up")
    ff = _matmul(hid, w_down.astype(BF16), tm=1024, tn=1024, tk=D_MODEL, out_dtype=F32, name="mlp_down")
    return _res_final(ff, h, norm_mlp_post)


def kernel(x, norm_mix_pre, w_in, conv_qkv_w, a_log, dt_bias, gdn_norm_w, conv_sc_w, w_out, norm_mix_post,
           norm_mlp_pre, w_up, w_down, norm_mlp_post):
    bsz, seq, d = x.shape
    h = x.reshape(bsz * seq, d)
    for l in range(norm_mix_pre.shape[0]):
        h = _layer(h, norm_mix_pre[l], w_in[l], conv_qkv_w[l], a_log[l], dt_bias[l], gdn_norm_w[l], conv_sc_w[l],
                   w_out[l], norm_mix_post[l], norm_mlp_pre[l], w_up[l], w_down[l], norm_mlp_post[l],
                   bsz=bsz, seq=seq)
    return h.reshape(bsz, seq, d)
```

```python
import functools

import jax
import jax.numpy as jnp
from jax import lax
from jax.experimental import pallas as pl
from jax.experimental.pallas import tpu as pltpu

F32 = jnp.float32
BF16 = jnp.bfloat16

D_MODEL = 4096
CHUNK = 64
HEAD_DIM = 128
GDN_WIDTH = 2048
GDN_HEADS = GDN_WIDTH // HEAD_DIM
SC_WIDTH = 2048
GDN_CONV = 4
SC_CONV = 3
NORM_EPS = 1e-6
L2_EPS = 1e-6

V7X_VMEM_LIMIT_BYTES = 56 * 1024 * 1024


def _params(*sem):
    return pltpu.CompilerParams(dimension_semantics=sem, vmem_limit_bytes=V7X_VMEM_LIMIT_BYTES)


def _sigmoid(x):
    return 1.0 / (1.0 + jnp.exp(-x))


def _rmsnorm_kernel(x_ref, g_ref, o_ref):
    x = x_ref[...]
    ms = jnp.mean(x * x, axis=-1, keepdims=True)
    o_ref[...] = (x * lax.rsqrt(ms + NORM_EPS) * g_ref[...]).astype(o_ref.dtype)


def _rmsnorm(x, g, tm=256):
    m, d = x.shape
    return pl.pallas_call(
        _rmsnorm_kernel,
        grid=(m // tm,),
        in_specs=[pl.BlockSpec((tm, d), lambda i: (i, 0)), pl.BlockSpec((1, d), lambda i: (0, 0))],
        out_specs=pl.BlockSpec((tm, d), lambda i: (i, 0)),
        out_shape=jax.ShapeDtypeStruct((m, d), BF16),
        compiler_params=_params("parallel"),
        name="rmsnorm",
    )(x, g.reshape(1, d))


def _mm_kernel(a_ref, w_ref, o_ref, *scratch, nk, act):
    part = jnp.dot(a_ref[...], w_ref[...].astype(BF16), preferred_element_type=F32)

    def finish(acc):
        if act == "relu2":
            r = jnp.maximum(acc, 0.0)
            acc = r * r
        o_ref[...] = acc.astype(o_ref.dtype)

    if nk == 1:
        finish(part)
        return
    (acc_ref,) = scratch
    k = pl.program_id(2)

    @pl.when(k == 0)
    def _():
        acc_ref[...] = part

    @pl.when(jnp.logical_and(k > 0, k < nk - 1))
    def _():
        acc_ref[...] += part

    @pl.when(k == nk - 1)
    def _():
        finish(acc_ref[...] + part)


def _a_buffering(nk):
    return pl.Buffered(1) if nk == 1 else None


def _matmul(a, w, *, tm, tn, tk, out_dtype, act=None, name="matmul"):
    m, kdim = a.shape
    _, n = w.shape
    nk = kdim // tk
    scratch = [pltpu.VMEM((tm, tn), F32)] if nk > 1 else []
    return pl.pallas_call(
        functools.partial(_mm_kernel, nk=nk, act=act),
        grid=(m // tm, n // tn, nk),
        in_specs=[pl.BlockSpec((tm, tk), lambda i, j, k: (i, k), pipeline_mode=_a_buffering(nk)),
                  pl.BlockSpec((tk, tn), lambda i, j, k: (k, j))],
        out_specs=pl.BlockSpec((tm, tn), lambda i, j, k: (i, j)),
        out_shape=jax.ShapeDtypeStruct((m, n), out_dtype),
        scratch_shapes=scratch,
        compiler_params=_params("parallel", "parallel", "arbitrary"),
        name=name,
    )(a, w)


def _mm_pair_kernel(a1_ref, a2_ref, w1_ref, w2_ref, o_ref):
    acc = jnp.dot(a1_ref[...], w1_ref[...].astype(BF16), preferred_element_type=F32)
    acc = acc + jnp.dot(a2_ref[...], w2_ref[...].astype(BF16), preferred_element_type=F32)
    o_ref[...] = acc.astype(o_ref.dtype)


def _matmul_pair(a1, a2, w, *, tm, tn, out_dtype, name):
    m, k1 = a1.shape
    _, k2 = a2.shape
    assert k1 == k2 and w.shape[0] == k1 + k2
    n = w.shape[1]
    return pl.pallas_call(
        _mm_pair_kernel,
        grid=(m // tm, n // tn),
        in_specs=[pl.BlockSpec((tm, k1), lambda i, j: (i, 0), pipeline_mode=pl.Buffered(1)),
                  pl.BlockSpec((tm, k2), lambda i, j: (i, 0), pipeline_mode=pl.Buffered(1)),
                  pl.BlockSpec((k1, tn), lambda i, j: (0, j)), pl.BlockSpec((k2, tn), lambda i, j: (1, j))],
        out_specs=pl.BlockSpec((tm, tn), lambda i, j: (i, j)),
        out_shape=jax.ShapeDtypeStruct((m, n), out_dtype),
        compiler_params=_params("parallel", "parallel"),
        name=name,
    )(a1, a2, w, w)


def _gate_kernel(a_ref, w_ref, alog_ref, dtb_ref, g_ref, beta_ref):
    p = jnp.dot(a_ref[...], w_ref[...], preferred_element_type=F32)
    a = p[:, :128] + dtb_ref[...]
    softplus = jnp.maximum(a, 0.0) + jnp.log(1.0 + jnp.exp(-jnp.abs(a)))
    g_ref[...] = -jnp.exp(alog_ref[...]) * softplus
    beta_ref[...] = _sigmoid(p[:, 128:])


def _gate_proj(xn, w_ab, a_log, dt_bias, tm=1024):
    m, d = xn.shape
    row = lambda i: (i, 0)
    fixed = lambda i: (0, 0)
    return pl.pallas_call(
        _gate_kernel,
        grid=(m // tm,),
        in_specs=[pl.BlockSpec((tm, d), row), pl.BlockSpec((d, 256), fixed),
                  pl.BlockSpec((1, 128), fixed), pl.BlockSpec((1, 128), fixed)],
        out_specs=[pl.BlockSpec((tm, 128), row), pl.BlockSpec((tm, 128), row)],
        out_shape=[jax.ShapeDtypeStruct((m, 128), F32), jax.ShapeDtypeStruct((m, 128), F32)],
        compiler_params=_params("parallel"),
        name="gate_proj",
    )(xn, w_ab, a_log, dt_bias)


def _gdn_kernel(q_ref, k_ref, v_ref, z_ref, qh_ref, kh_ref, vh_ref, g_ref, b_ref, wq_ref, wk_ref, wv_ref, nw_ref,
                o_ref, s_ref, u_ref, wqd_ref, kdt_ref, qkm_ref, egl_ref, *, hb, nchunk):
    head0 = pl.program_id(1) * hb
    tile = pl.program_id(2)

    @pl.when(tile == 0)
    def _():
        s_ref[...] = jnp.zeros_like(s_ref)

    row = lax.broadcasted_iota(jnp.int32, (CHUNK, CHUNK), 0)
    col = lax.broadcasted_iota(jnp.int32, (CHUNK, CHUNK), 1)
    tri_incl = row >= col
    tri_strict = row > col
    tri_f = tri_incl.astype(F32)
    eye = (row == col).astype(F32)
    shift = (128 - head0) % 128
    nt = (((1,), (1,)), ((), ()))
    keep_halo = jnp.where(tile > 0, 1.0, 0.0)

    def precompute(c, carry):
        r0 = pl.multiple_of(c * CHUNK, CHUNK)
        rp = pl.multiple_of(jnp.maximum(r0 - 8, 0), 8)
        first = c == 0
        rows = pl.ds(r0, CHUNK)

        graw = pltpu.roll(g_ref[0, rows, :], shift, axis=1)
        beta = pltpu.roll(b_ref[0, rows, :], shift, axis=1)
        gc = jnp.dot(tri_f, graw, preferred_element_type=F32, precision=lax.Precision.HIGHEST)
        eg = jnp.exp(gc)
        g_last = gc[CHUNK - 1:CHUNK, :]
        eg_rest = jnp.exp(g_last - gc)
        egl_ref[c] = jnp.broadcast_to(jnp.exp(g_last), (8, 128))
        gc_t = gc.T

        def conv_silu(ref, halo_ref, w_ref, lanes):
            cur = ref[0, rows, lanes]
            prev = jnp.where(first, halo_ref[0, :, lanes] * keep_halo, ref[0, pl.ds(rp, 8), lanes])
            ext = jnp.concatenate([prev, cur], axis=0)
            w = w_ref[:, lanes]
            acc = cur * w[GDN_CONV - 1:GDN_CONV, :]
            for t in range(GDN_CONV - 1):
                off = 8 - (GDN_CONV - 1) + t
                acc = acc + ext[off:off + CHUNK, :] * w[t:t + 1, :]
            return acc * _sigmoid(acc)

        heads = range(hb)
        lanes = [slice(j * HEAD_DIM, (j + 1) * HEAD_DIM) for j in heads]
        q = [conv_silu(q_ref, qh_ref, wq_ref, lanes[j]) for j in heads]
        k = [conv_silu(k_ref, kh_ref, wk_ref, lanes[j]) for j in heads]
        v = [conv_silu(v_ref, vh_ref, wv_ref, lanes[j]) for j in heads]
        q = [x * (lax.rsqrt(jnp.sum(x * x, axis=-1, keepdims=True) + L2_EPS) * (HEAD_DIM ** -0.5)) for x in q]
        k = [x * lax.rsqrt(jnp.sum(x * x, axis=-1, keepdims=True) + L2_EPS) for x in k]
        kb = [x.astype(BF16) for x in k]
        kk = [lax.dot_general(kb[j], kb[j], nt, preferred_element_type=F32) for j in heads]
        qk = [lax.dot_general(q[j].astype(BF16), kb[j], nt, preferred_element_type=F32) for j in heads]

        beta_col = [beta[:, j:j + 1] for j in heads]
        decay = [jnp.where(tri_incl, jnp.exp(gc[:, j:j + 1] - gc_t[j:j + 1, :]), 0.0) for j in heads]
        low = [jnp.where(tri_strict, beta_col[j] * kk[j] * decay[j], 0.0) for j in heads]

        inv = [eye - x for x in low]
        pw = low
        for _ in range(5):
            pwb = [x.astype(BF16) for x in pw]
            pw = [jnp.dot(x, x, preferred_element_type=F32) for x in pwb]
            inv = [inv[j] + jnp.dot(inv[j].astype(BF16), pw[j].astype(BF16), preferred_element_type=F32)
                   for j in heads]

        rhs = [jnp.concatenate([v[j] * beta_col[j], k[j] * (beta_col[j] * eg[:, j:j + 1])], axis=1) for j in heads]
        sol = [jnp.dot(inv[j].astype(BF16), rhs[j].astype(BF16), preferred_element_type=F32) for j in heads]
        for j in heads:
            u_ref[j, rows, :] = sol[j][:, :HEAD_DIM]
            wqd_ref[j, c, :CHUNK, :] = sol[j][:, HEAD_DIM:].astype(BF16)
            wqd_ref[j, c, CHUNK:, :] = (q[j] * eg[:, j:j + 1]).astype(BF16)
            kdt_ref[j, c] = (k[j] * eg_rest[:, j:j + 1]).T.astype(BF16)
            qkm_ref[j, rows, :] = (qk[j] * decay[j]).astype(BF16)
        return carry

    def scan(c, carry):
        r0 = pl.multiple_of(c * CHUNK, CHUNK)
        rows = pl.ds(r0, CHUNK)
        egl = egl_ref[c]
        heads = range(hb)
        state = [s_ref[j] for j in heads]
        proj = [jnp.dot(wqd_ref[j, c], state[j].astype(BF16), preferred_element_type=F32) for j in heads]
        v_new = [(u_ref[j, rows, :] - proj[j][:CHUNK]).astype(BF16) for j in heads]
        for j in heads:
            s_ref[j] = state[j] * egl[0:1, j:j + 1] + jnp.dot(kdt_ref[j, c], v_new[j], preferred_element_type=F32)
        o = [proj[j][CHUNK:] + jnp.dot(qkm_ref[j, rows, :], v_new[j], preferred_element_type=F32) for j in heads]
        for j in heads:
            lanes = slice(j * HEAD_DIM, (j + 1) * HEAD_DIM)
            ms = jnp.mean(o[j] * o[j], axis=-1, keepdims=True)
            z = z_ref[0, rows, lanes]
            out = o[j] * lax.rsqrt(ms + NORM_EPS) * nw_ref[...] * (z * _sigmoid(z))
            o_ref[0, rows, lanes] = out.astype(o_ref.dtype)
        return carry

    lax.fori_loop(0, nchunk, precompute, 0)
    lax.fori_loop(0, nchunk, scan, 0)


def _gdn(proj, g, beta, conv_w, norm_w, *, bsz, seq, hb=8, ts=512):
    nblk = GDN_WIDTH // (HEAD_DIM * hb)
    wblk = HEAD_DIM * hb
    nchunk = ts // CHUNK

    def col(group):
        return pl.BlockSpec((1, ts, wblk), lambda b, h, t: (b, t, group * nblk + h))

    def halo(group):
        return pl.BlockSpec((1, 8, wblk), lambda b, h, t: (b, jnp.maximum(t * (ts // 8) - 1, 0), group * nblk + h))

    def wcol(group):
        return pl.BlockSpec((GDN_CONV, wblk), lambda b, h, t: (0, group * nblk + h))

    gspec = pl.BlockSpec((1, ts, 128), lambda b, h, t: (b, t, 0))
    return pl.pallas_call(
        functools.partial(_gdn_kernel, hb=hb, nchunk=nchunk),
        grid=(bsz, nblk, seq // ts),
        in_specs=[col(0), col(1), col(2), col(3), halo(0), halo(1), halo(2), gspec, gspec,
                  wcol(0), wcol(1), wcol(2), pl.BlockSpec((1, HEAD_DIM), lambda b, h, t: (0, 0))],
        out_specs=pl.BlockSpec((1, ts, wblk), lambda b, h, t: (b, t, h)),
        out_shape=jax.ShapeDtypeStruct((bsz, seq, GDN_WIDTH), BF16),
        scratch_shapes=[pltpu.VMEM((hb, HEAD_DIM, HEAD_DIM), F32),
                        pltpu.VMEM((hb, ts, HEAD_DIM), F32),
                        pltpu.VMEM((hb, nchunk, 2 * CHUNK, HEAD_DIM), BF16),
                        pltpu.VMEM((hb, nchunk, HEAD_DIM, CHUNK), BF16),
                        pltpu.VMEM((hb, ts, CHUNK), BF16),
                        pltpu.VMEM((nchunk, 8, 128), F32)],
        compiler_params=_params("parallel", "parallel", "arbitrary"),
        name="gdn",
    )(proj, proj, proj, proj, proj, proj, proj, g, beta, conv_w, conv_w, conv_w, norm_w)


def _sconv_kernel(b_ref, c_ref, h_ref, w_ref, o_ref, *, rows_per_step, nstep):
    w = w_ref[...]

    def step(i, carry):
        r0 = pl.multiple_of(i * rows_per_step, rows_per_step)
        rp = pl.multiple_of(jnp.maximum(r0 - 8, 0), 8)
        keep = jnp.where(i > 0, 1.0, 0.0)
        rows = pl.ds(r0, rows_per_step)
        cur = c_ref[0, rows, :] * h_ref[0, rows, :]
        prev = c_ref[0, pl.ds(rp, 8), :] * h_ref[0, pl.ds(rp, 8), :] * keep
        ext = jnp.concatenate([prev, cur], axis=0)
        acc = cur * w[SC_CONV - 1:SC_CONV, :]
        for t in range(SC_CONV - 1):
            off = 8 - (SC_CONV - 1) + t
            acc = acc + ext[off:off + rows_per_step, :] * w[t:t + 1, :]
        o_ref[0, rows, :] = (b_ref[0, rows, :] * acc).astype(o_ref.dtype)
        return carry

    lax.fori_loop(0, nstep, step, 0)


def _sconv(proj, conv_w, *, bsz, seq, col0, tc=256, rows_per_step=64):
    nblk = SC_WIDTH // tc
    base = col0 // tc

    def col(group):
        return pl.BlockSpec((1, seq, tc), lambda b, j: (b, 0, base + group * nblk + j))

    return pl.pallas_call(
        functools.partial(_sconv_kernel, rows_per_step=rows_per_step, nstep=seq // rows_per_step),
        grid=(bsz, nblk),
        in_specs=[col(0), col(1), col(2), pl.BlockSpec((SC_CONV, tc), lambda b, j: (0, j))],
        out_specs=pl.BlockSpec((1, seq, tc), lambda b, j: (b, 0, j)),
        out_shape=jax.ShapeDtypeStruct((bsz, seq, SC_WIDTH), BF16),
        compiler_params=_params("parallel", "parallel"),
        name="sconv",
    )(proj, proj, proj, conv_w)


def _res_norm_kernel(t_ref, r_ref, gpost_ref, gnext_ref, h_ref, hn_ref):
    t = t_ref[...]
    h = r_ref[...] + t * lax.rsqrt(jnp.mean(t * t, axis=-1, keepdims=True) + NORM_EPS) * gpost_ref[...]
    h_ref[...] = h
    hn = h * lax.rsqrt(jnp.mean(h * h, axis=-1, keepdims=True) + NORM_EPS) * gnext_ref[...]
    hn_ref[...] = hn.astype(hn_ref.dtype)


def _res_norm(t, resid, g_post, g_next, tm=256):
    m, d = t.shape
    row = lambda i: (i, 0)
    fixed = lambda i: (0, 0)
    return pl.pallas_call(
        _res_norm_kernel,
        grid=(m // tm,),
        in_specs=[pl.BlockSpec((tm, d), row), pl.BlockSpec((tm, d), row),
                  pl.BlockSpec((1, d), fixed), pl.BlockSpec((1, d), fixed)],
        out_specs=[pl.BlockSpec((tm, d), row), pl.BlockSpec((tm, d), row)],
        out_shape=[jax.ShapeDtypeStruct((m, d), F32), jax.ShapeDtypeStruct((m, d), BF16)],
        compiler_params=_params("parallel"),
        name="res_norm",
    )(t, resid, g_post.reshape(1, d), g_next.reshape(1, d))


def _res_final_kernel(t_ref, r_ref, g_ref, o_ref):
    t = t_ref[...]
    o_ref[...] = r_ref[...] + t * lax.rsqrt(jnp.mean(t * t, axis=-1, keepdims=True) + NORM_EPS) * g_ref[...]


def _res_final(t, resid, g, tm=256):
    m, d = t.shape
    row = lambda i: (i, 0)
    return pl.pallas_call(
        _res_final_kernel,
        grid=(m // tm,),
        in_specs=[pl.BlockSpec((tm, d), row), pl.BlockSpec((tm, d), row), pl.BlockSpec((1, d), lambda i: (0, 0))],
        out_specs=pl.BlockSpec((tm, d), row),
        out_shape=jax.ShapeDtypeStruct((m, d), F32),
        compiler_params=_params("parallel"),
        name="res_final",
    )(t, resid, g.reshape(1, d))


def _layer(h, norm_mix_pre, w_in, conv_qkv_w, a_log, dt_bias, gdn_norm_w, conv_sc_w, w_out,
           norm_mix_post, norm_mlp_pre, w_up, w_down, norm_mlp_post, *, bsz, seq):
    m = bsz * seq
    n_gdn = 4 * GDN_WIDTH
    n_ab = 2 * GDN_HEADS

    xn = _rmsnorm(h, norm_mix_pre)

    w_main = jnp.concatenate([w_in[:, :n_gdn], w_in[:, n_gdn + n_ab:]], axis=1).astype(BF16)
    pad = jnp.zeros((D_MODEL, 128 - GDN_HEADS), F32)
    w_ab = jnp.concatenate([w_in[:, n_gdn:n_gdn + GDN_HEADS], pad,
                            w_in[:, n_gdn + GDN_HEADS:n_gdn + n_ab], pad], axis=1).astype(BF16)
    lane_pad = jnp.zeros((1, 128 - GDN_HEADS), F32)
    a_log_p = jnp.concatenate([a_log.reshape(1, -1), lane_pad], axis=1)
    dt_bias_p = jnp.concatenate([dt_bias.reshape(1, -1), lane_pad], axis=1)

    proj = _matmul(xn, w_main, tm=1024, tn=1024, tk=D_MODEL, out_dtype=F32, name="in_proj")
    g, beta = _gate_proj(xn, w_ab, a_log_p, dt_bias_p)

    proj3 = proj.reshape(bsz, seq, -1)
    gdn_out = _gdn(proj3, g.reshape(bsz, seq, 128), beta.reshape(bsz, seq, 128), conv_qkv_w,
                   gdn_norm_w.reshape(1, HEAD_DIM), bsz=bsz, seq=seq)
    sc_out = _sconv(proj3, conv_sc_w, bsz=bsz, seq=seq, col0=n_gdn)

    mix = _matmul_pair(gdn_out.reshape(m, -1), sc_out.reshape(m, -1), w_out, tm=2048, tn=512,
                       out_dtype=F32, name="out_proj")
    h, hn = _res_norm(mix, h, norm_mix_post, norm_mlp_pre)

    hid = _matmul(hn, w_up, tm=2048, tn=512, tk=D_MODEL, out_dtype=BF16, act="relu2", name="mlp_up")
    ff = _matmul(hid, w_down, tm=2048, tn=1024, tk=1024, out_dtype=F32, name="mlp_down")
    return _res_final(ff, h, norm_mlp_post)


def kernel(x, norm_mix_pre, w_in, conv_qkv_w, a_log, dt_bias, gdn_norm_w, conv_sc_w, w_out, norm_mix_post,
           norm_mlp_pre, w_up, w_down, norm_mlp_post):
    bsz, seq, d = x.shape
    h = x.reshape(bsz * seq, d)
    for l in range(norm_mix_pre.shape[0]):
        h = _layer(h, norm_mix_pre[l], w_in[l], conv_qkv_w[l], a_log[l], dt_bias[l], gdn_norm_w[l], conv_sc_w[l],
                   w_out[l], norm_mix_post[l], norm_mlp_pre[l], w_up[l], w_down[l], norm_mlp_post[l],
                   bsz=bsz, seq=seq)
    return h.reshape(bsz, seq, d)
```

```python
import functools

import jax
import jax.numpy as jnp
from jax import lax
from jax.experimental import pallas as pl
from jax.experimental.pallas import tpu as pltpu

F32 = jnp.float32
BF16 = jnp.bfloat16

D_MODEL = 4096
CHUNK = 64
HEAD_DIM = 128
GDN_WIDTH = 2048
GDN_HEADS = GDN_WIDTH // HEAD_DIM
SC_WIDTH = 2048
GDN_CONV = 4
SC_CONV = 3
NORM_EPS = 1e-6
L2_EPS = 1e-6

V7X_VMEM_LIMIT_BYTES = 56 * 1024 * 1024


def _params(*sem):
    return pltpu.CompilerParams(dimension_semantics=sem, vmem_limit_bytes=V7X_VMEM_LIMIT_BYTES)


def _sigmoid(x):
    return 1.0 / (1.0 + jnp.exp(-x))


def _rmsnorm_kernel(x_ref, g_ref, o_ref):
    x = x_ref[...]
    ms = jnp.mean(x * x, axis=-1, keepdims=True)
    o_ref[...] = (x * lax.rsqrt(ms + NORM_EPS) * g_ref[...]).astype(o_ref.dtype)


def _rmsnorm(x, g, tm=256):
    m, d = x.shape
    return pl.pallas_call(
        _rmsnorm_kernel,
        grid=(m // tm,),
        in_specs=[pl.BlockSpec((tm, d), lambda i: (i, 0)), pl.BlockSpec((1, d), lambda i: (0, 0))],
        out_specs=pl.BlockSpec((tm, d), lambda i: (i, 0)),
        out_shape=jax.ShapeDtypeStruct((m, d), BF16),
        compiler_params=_params("parallel"),
        name="rmsnorm",
    )(x, g.reshape(1, d))


def _mm_kernel(a_ref, w_ref, o_ref, *, nk, act):
    def part():
        return jnp.dot(a_ref[...], w_ref[...].astype(BF16), preferred_element_type=F32)

    if nk == 1:
        acc = part()
        if act == "relu2":
            r = jnp.maximum(acc, 0.0)
            acc = r * r
        o_ref[...] = acc.astype(o_ref.dtype)
        return
    k = pl.program_id(2)

    @pl.when(k == 0)
    def _():
        o_ref[...] = part()

    @pl.when(k > 0)
    def _():
        o_ref[...] += part()


def _a_buffering(nk):
    return pl.Buffered(1) if nk == 1 else None


def _matmul(a, w, *, tm, tn, tk, out_dtype, n_out=None, act=None, name="matmul"):
    m, kdim = a.shape
    n = w.shape[1] if n_out is None else n_out
    nk = kdim // tk
    assert nk == 1 or (act is None and out_dtype == F32)
    return pl.pallas_call(
        functools.partial(_mm_kernel, nk=nk, act=act),
        grid=(m // tm, n // tn, nk),
        in_specs=[pl.BlockSpec((tm, tk), lambda i, j, k: (i, k), pipeline_mode=_a_buffering(nk)),
                  pl.BlockSpec((tk, tn), lambda i, j, k: (k, j))],
        out_specs=pl.BlockSpec((tm, tn), lambda i, j, k: (i, j)),
        out_shape=jax.ShapeDtypeStruct((m, n), out_dtype),
        compiler_params=_params("parallel", "parallel", "arbitrary"),
        name=name,
    )(a, w)


def _mm_pair_kernel(a1_ref, a2_ref, w1_ref, w2_ref, o_ref):
    acc = jnp.dot(a1_ref[...], w1_ref[...].astype(BF16), preferred_element_type=F32)
    acc = acc + jnp.dot(a2_ref[...], w2_ref[...].astype(BF16), preferred_element_type=F32)
    o_ref[...] = acc.astype(o_ref.dtype)


def _matmul_pair(a1, a2, w, *, tm, tn, out_dtype, name):
    m, k1 = a1.shape
    _, k2 = a2.shape
    assert k1 == k2 and w.shape[0] == k1 + k2
    n = w.shape[1]
    return pl.pallas_call(
        _mm_pair_kernel,
        grid=(m // tm, n // tn),
        in_specs=[pl.BlockSpec((tm, k1), lambda i, j: (i, 0), pipeline_mode=pl.Buffered(1)),
                  pl.BlockSpec((tm, k2), lambda i, j: (i, 0), pipeline_mode=pl.Buffered(1)),
                  pl.BlockSpec((k1, tn), lambda i, j: (0, j)), pl.BlockSpec((k2, tn), lambda i, j: (1, j))],
        out_specs=pl.BlockSpec((tm, tn), lambda i, j: (i, j)),
        out_shape=jax.ShapeDtypeStruct((m, n), out_dtype),
        compiler_params=_params("parallel", "parallel"),
        name=name,
    )(a1, a2, w, w)


def _gate_kernel(a_ref, w_ref, alog_ref, dtb_ref, g_ref, beta_ref):
    p = jnp.dot(a_ref[...], w_ref[...], preferred_element_type=F32)
    a = p[:, :128] + dtb_ref[...]
    softplus = jnp.maximum(a, 0.0) + jnp.log(1.0 + jnp.exp(-jnp.abs(a)))
    g_ref[...] = -jnp.exp(alog_ref[...]) * softplus
    beta_ref[...] = _sigmoid(p[:, 128:])


def _gate_proj(xn, w_ab, a_log, dt_bias, tm=1024):
    m, d = xn.shape
    row = lambda i: (i, 0)
    fixed = lambda i: (0, 0)
    return pl.pallas_call(
        _gate_kernel,
        grid=(m // tm,),
        in_specs=[pl.BlockSpec((tm, d), row), pl.BlockSpec((d, 256), fixed),
                  pl.BlockSpec((1, 128), fixed), pl.BlockSpec((1, 128), fixed)],
        out_specs=[pl.BlockSpec((tm, 128), row), pl.BlockSpec((tm, 128), row)],
        out_shape=[jax.ShapeDtypeStruct((m, 128), F32), jax.ShapeDtypeStruct((m, 128), F32)],
        compiler_params=_params("parallel"),
        name="gate_proj",
    )(xn, w_ab, a_log, dt_bias)


def _gdn_kernel(q_ref, k_ref, v_ref, z_ref, qh_ref, kh_ref, vh_ref, g_ref, b_ref, wq_ref, wk_ref, wv_ref, nw_ref,
                o_ref, s_ref, u_ref, wqd_ref, kdt_ref, qkm_ref, egl_ref, *, hb, nchunk):
    head0 = pl.program_id(1) * hb
    tile = pl.program_id(2)

    @pl.when(tile == 0)
    def _():
        s_ref[...] = jnp.zeros_like(s_ref)

    row = lax.broadcasted_iota(jnp.int32, (CHUNK, CHUNK), 0)
    col = lax.broadcasted_iota(jnp.int32, (CHUNK, CHUNK), 1)
    tri_incl = row >= col
    tri_strict = row > col
    tri_f = tri_incl.astype(F32)
    eye = (row == col).astype(F32)
    shift = (128 - head0) % 128
    nt = (((1,), (1,)), ((), ()))
    keep_halo = jnp.where(tile > 0, 1.0, 0.0)

    def precompute(c, carry):
        r0 = pl.multiple_of(c * CHUNK, CHUNK)
        rp = pl.multiple_of(jnp.maximum(r0 - 8, 0), 8)
        first = c == 0
        rows = pl.ds(r0, CHUNK)

        graw = pltpu.roll(g_ref[0, rows, :], shift, axis=1)
        beta = pltpu.roll(b_ref[0, rows, :], shift, axis=1)
        gc = jnp.dot(tri_f, graw, preferred_element_type=F32, precision=lax.Precision.HIGHEST)
        eg = jnp.exp(gc)
        g_last = gc[CHUNK - 1:CHUNK, :]
        eg_rest = jnp.exp(g_last - gc)
        egl_ref[c] = jnp.broadcast_to(jnp.exp(g_last), (8, 128))
        gc_t = gc.T

        def conv_silu(ref, halo_ref, w_ref, lanes):
            cur = ref[0, rows, lanes]
            prev = jnp.where(first, halo_ref[0, :, lanes] * keep_halo, ref[0, pl.ds(rp, 8), lanes])
            ext = jnp.concatenate([prev, cur], axis=0)
            w = w_ref[:, lanes]
            acc = cur * w[GDN_CONV - 1:GDN_CONV, :]
            for t in range(GDN_CONV - 1):
                off = 8 - (GDN_CONV - 1) + t
                acc = acc + ext[off:off + CHUNK, :] * w[t:t + 1, :]
            return acc * _sigmoid(acc)

        heads = range(hb)
        lanes = [slice(j * HEAD_DIM, (j + 1) * HEAD_DIM) for j in heads]
        q = [conv_silu(q_ref, qh_ref, wq_ref, lanes[j]) for j in heads]
        k = [conv_silu(k_ref, kh_ref, wk_ref, lanes[j]) for j in heads]
        v = [conv_silu(v_ref, vh_ref, wv_ref, lanes[j]) for j in heads]
        q = [x * (lax.rsqrt(jnp.sum(x * x, axis=-1, keepdims=True) + L2_EPS) * (HEAD_DIM ** -0.5)) for x in q]
        k = [x * lax.rsqrt(jnp.sum(x * x, axis=-1, keepdims=True) + L2_EPS) for x in k]
        kb = [x.astype(BF16) for x in k]
        kk = [lax.dot_general(kb[j], kb[j], nt, preferred_element_type=F32) for j in heads]
        qk = [lax.dot_general(q[j].astype(BF16), kb[j], nt, preferred_element_type=F32) for j in heads]

        beta_col = [beta[:, j:j + 1] for j in heads]
        decay = [jnp.where(tri_incl, jnp.exp(gc[:, j:j + 1] - gc_t[j:j + 1, :]), 0.0) for j in heads]
        low = [jnp.where(tri_strict, beta_col[j] * kk[j] * decay[j], 0.0) for j in heads]

        inv = [eye - x for x in low]
        pw = low
        for _ in range(5):
            pwb = [x.astype(BF16) for x in pw]
            pw = [jnp.dot(x, x, preferred_element_type=F32) for x in pwb]
            inv = [inv[j] + jnp.dot(inv[j].astype(BF16), pw[j].astype(BF16), preferred_element_type=F32)
                   for j in heads]

        rhs = [jnp.concatenate([v[j] * beta_col[j], k[j] * (beta_col[j] * eg[:, j:j + 1])], axis=1) for j in heads]
        sol = [jnp.dot(inv[j].astype(BF16), rhs[j].astype(BF16), preferred_element_type=F32) for j in heads]
        for j in heads:
            u_ref[j, rows, :] = sol[j][:, :HEAD_DIM]
            wqd_ref[j, c, :CHUNK, :] = sol[j][:, HEAD_DIM:].astype(BF16)
            wqd_ref[j, c, CHUNK:, :] = (q[j] * eg[:, j:j + 1]).astype(BF16)
            kdt_ref[j, c] = (k[j] * eg_rest[:, j:j + 1]).T.astype(BF16)
            qkm_ref[j, rows, :] = (qk[j] * decay[j]).astype(BF16)
        return carry

    def scan(c, carry):
        r0 = pl.multiple_of(c * CHUNK, CHUNK)
        rows = pl.ds(r0, CHUNK)
        egl = egl_ref[c]
        heads = range(hb)
        state = [s_ref[j] for j in heads]
        proj = [jnp.dot(wqd_ref[j, c], state[j].astype(BF16), preferred_element_type=F32) for j in heads]
        v_new = [(u_ref[j, rows, :] - proj[j][:CHUNK]).astype(BF16) for j in heads]
        for j in heads:
            s_ref[j] = state[j] * egl[0:1, j:j + 1] + jnp.dot(kdt_ref[j, c], v_new[j], preferred_element_type=F32)
        o = [proj[j][CHUNK:] + jnp.dot(qkm_ref[j, rows, :], v_new[j], preferred_element_type=F32) for j in heads]
        for j in heads:
            lanes = slice(j * HEAD_DIM, (j + 1) * HEAD_DIM)
            ms = jnp.mean(o[j] * o[j], axis=-1, keepdims=True)
            z = z_ref[0, rows, lanes]
            out = o[j] * lax.rsqrt(ms + NORM_EPS) * nw_ref[...] * (z * _sigmoid(z))
            o_ref[0, rows, lanes] = out.astype(o_ref.dtype)
        return carry

    lax.fori_loop(0, nchunk, precompute, 0)
    lax.fori_loop(0, nchunk, scan, 0)


def _gdn(proj, g, beta, conv_w, norm_w, *, bsz, seq, hb=8, ts=512):
    nblk = GDN_WIDTH // (HEAD_DIM * hb)
    wblk = HEAD_DIM * hb
    nchunk = ts // CHUNK

    def col(group):
        return pl.BlockSpec((1, ts, wblk), lambda b, h, t: (b, t, group * nblk + h))

    def halo(group):
        return pl.BlockSpec((1, 8, wblk), lambda b, h, t: (b, jnp.maximum(t * (ts // 8) - 1, 0), group * nblk + h))

    def wcol(group):
        return pl.BlockSpec((GDN_CONV, wblk), lambda b, h, t: (0, group * nblk + h))

    gspec = pl.BlockSpec((1, ts, 128), lambda b, h, t: (b, t, 0))
    return pl.pallas_call(
        functools.partial(_gdn_kernel, hb=hb, nchunk=nchunk),
        grid=(bsz, nblk, seq // ts),
        in_specs=[col(0), col(1), col(2), col(3), halo(0), halo(1), halo(2), gspec, gspec,
                  wcol(0), wcol(1), wcol(2), pl.BlockSpec((1, HEAD_DIM), lambda b, h, t: (0, 0))],
        out_specs=pl.BlockSpec((1, ts, wblk), lambda b, h, t: (b, t, h)),
        out_shape=jax.ShapeDtypeStruct((bsz, seq, GDN_WIDTH), BF16),
        scratch_shapes=[pltpu.VMEM((hb, HEAD_DIM, HEAD_DIM), F32),
                        pltpu.VMEM((hb, ts, HEAD_DIM), F32),
                        pltpu.VMEM((hb, nchunk, 2 * CHUNK, HEAD_DIM), BF16),
                        pltpu.VMEM((hb, nchunk, HEAD_DIM, CHUNK), BF16),
                        pltpu.VMEM((hb, ts, CHUNK), BF16),
                        pltpu.VMEM((nchunk, 8, 128), F32)],
        compiler_params=_params("parallel", "parallel", "arbitrary"),
        name="gdn",
    )(proj, proj, proj, proj, proj, proj, proj, g, beta, conv_w, conv_w, conv_w, norm_w)


def _sc_kernel(a_ref, wb_ref, wc_ref, wh_ref, cw_ref, o_ref, p_ref, b_ref, *, rows_per_step, nstep):
    p_ref[0:8, :] = jnp.zeros((8, p_ref.shape[1]), F32)
    p_ref[8:, :] = jnp.dot(a_ref[...], wc_ref[...], preferred_element_type=F32)
    p_ref[8:, :] *= jnp.dot(a_ref[...], wh_ref[...], preferred_element_type=F32)
    b_ref[...] = jnp.dot(a_ref[...], wb_ref[...], preferred_element_type=F32)
    w = cw_ref[...]

    def step(i, carry):
        r0 = pl.multiple_of(i * rows_per_step, rows_per_step)
        ext = p_ref[pl.ds(r0, rows_per_step + 8), :]
        acc = ext[8:, :] * w[SC_CONV - 1:SC_CONV, :]
        for t in range(SC_CONV - 1):
            off = 8 - (SC_CONV - 1) + t
            acc = acc + ext[off:off + rows_per_step, :] * w[t:t + 1, :]
        rows = pl.ds(r0, rows_per_step)
        o_ref[rows, :] = (b_ref[rows, :] * acc).astype(o_ref.dtype)
        return carry

    lax.fori_loop(0, nstep, step, 0)


def _sc_proj_conv(xn, w_sc, conv_w, *, seq, tc=256, rows_per_step=128):
    m, d = xn.shape
    nblk = SC_WIDTH // tc

    def wcol(group):
        return pl.BlockSpec((d, tc), lambda b, j: (0, group * nblk + j))

    return pl.pallas_call(
        functools.partial(_sc_kernel, rows_per_step=rows_per_step, nstep=seq // rows_per_step),
        grid=(m // seq, nblk),
        in_specs=[pl.BlockSpec((seq, d), lambda b, j: (b, 0), pipeline_mode=pl.Buffered(1)),
                  wcol(0), wcol(1), wcol(2), pl.BlockSpec((SC_CONV, tc), lambda b, j: (0, j))],
        out_specs=pl.BlockSpec((seq, tc), lambda b, j: (b, j)),
        out_shape=jax.ShapeDtypeStruct((m, SC_WIDTH), BF16),
        scratch_shapes=[pltpu.VMEM((8 + seq, tc), F32), pltpu.VMEM((seq, tc), F32)],
        compiler_params=_params("parallel", "parallel"),
        name="sc_proj_conv",
    )(xn, w_sc, w_sc, w_sc, conv_w)


def _res_norm_kernel(t_ref, r_ref, gpost_ref, gnext_ref, h_ref, hn_ref):
    t = t_ref[...]
    h = r_ref[...] + t * lax.rsqrt(jnp.mean(t * t, axis=-1, keepdims=True) + NORM_EPS) * gpost_ref[...]
    h_ref[...] = h
    hn = h * lax.rsqrt(jnp.mean(h * h, axis=-1, keepdims=True) + NORM_EPS) * gnext_ref[...]
    hn_ref[...] = hn.astype(hn_ref.dtype)


def _res_norm(t, resid, g_post, g_next, tm=256):
    m, d = t.shape
    row = lambda i: (i, 0)
    fixed = lambda i: (0, 0)
    return pl.pallas_call(
        _res_norm_kernel,
        grid=(m // tm,),
        in_specs=[pl.BlockSpec((tm, d), row), pl.BlockSpec((tm, d), row),
                  pl.BlockSpec((1, d), fixed), pl.BlockSpec((1, d), fixed)],
        out_specs=[pl.BlockSpec((tm, d), row), pl.BlockSpec((tm, d), row)],
        out_shape=[jax.ShapeDtypeStruct((m, d), F32), jax.ShapeDtypeStruct((m, d), BF16)],
        compiler_params=_params("parallel"),
        name="res_norm",
    )(t, resid, g_post.reshape(1, d), g_next.reshape(1, d))


def _res_final_kernel(t_ref, r_ref, g_ref, o_ref):
    t = t_ref[...]
    o_ref[...] = r_ref[...] + t * lax.rsqrt(jnp.mean(t * t, axis=-1, keepdims=True) + NORM_EPS) * g_ref[...]


def _res_final(t, resid, g, tm=256):
    m, d = t.shape
    row = lambda i: (i, 0)
    return pl.pallas_call(
        _res_final_kernel,
        grid=(m // tm,),
        in_specs=[pl.BlockSpec((tm, d), row), pl.BlockSpec((tm, d), row), pl.BlockSpec((1, d), lambda i: (0, 0))],
        out_specs=pl.BlockSpec((tm, d), row),
        out_shape=jax.ShapeDtypeStruct((m, d), F32),
        compiler_params=_params("parallel"),
        name="res_final",
    )(t, resid, g.reshape(1, d))


def _layer(h, norm_mix_pre, w_in, conv_qkv_w, a_log, dt_bias, gdn_norm_w, conv_sc_w, w_out,
           norm_mix_post, norm_mlp_pre, w_up, w_down, norm_mlp_post, *, bsz, seq):
    m = bsz * seq
    n_gdn = 4 * GDN_WIDTH
    n_ab = 2 * GDN_HEADS

    xn = _rmsnorm(h, norm_mix_pre)

    pad = jnp.zeros((D_MODEL, 128 - GDN_HEADS), F32)
    w_ab = jnp.concatenate([w_in[:, n_gdn:n_gdn + GDN_HEADS], pad,
                            w_in[:, n_gdn + GDN_HEADS:n_gdn + n_ab], pad], axis=1).astype(BF16)
    w_sc = w_in[:, n_gdn + n_ab:].astype(BF16)
    lane_pad = jnp.zeros((1, 128 - GDN_HEADS), F32)
    a_log_p = jnp.concatenate([a_log.reshape(1, -1), lane_pad], axis=1)
    dt_bias_p = jnp.concatenate([dt_bias.reshape(1, -1), lane_pad], axis=1)

    proj = _matmul(xn, w_in, tm=2048, tn=512, tk=D_MODEL, out_dtype=F32, n_out=n_gdn, name="in_proj")
    g, beta = _gate_proj(xn, w_ab, a_log_p, dt_bias_p)

    proj3 = proj.reshape(bsz, seq, -1)
    gdn_out = _gdn(proj3, g.reshape(bsz, seq, 128), beta.reshape(bsz, seq, 128), conv_qkv_w,
                   gdn_norm_w.reshape(1, HEAD_DIM), bsz=bsz, seq=seq)
    sc_out = _sc_proj_conv(xn, w_sc, conv_sc_w, seq=seq)

    mix = _matmul_pair(gdn_out.reshape(m, -1), sc_out, w_out, tm=2048, tn=512,
                       out_dtype=F32, name="out_proj")
    h, hn = _res_norm(mix, h, norm_mix_post, norm_mlp_pre)

    hid = _matmul(hn, w_up, tm=2048, tn=512, tk=D_MODEL, out_dtype=BF16, act="relu2", name="mlp_up")
    ff = _matmul(hid, w_down, tm=2048, tn=1024, tk=1024, out_dtype=F32, name="mlp_down")
    return _res_final(ff, h, norm_mlp_post)


def kernel(x, norm_mix_pre, w_in, conv_qkv_w, a_log, dt_bias, gdn_norm_w, conv_sc_w, w_out, norm_mix_post,
           norm_mlp_pre, w_up, w_down, norm_mlp_post):
    bsz, seq, d = x.shape
    h = x.reshape(bsz * seq, d)
    for l in range(norm_mix_pre.shape[0]):
        h = _layer(h, norm_mix_pre[l], w_in[l], conv_qkv_w[l], a_log[l], dt_bias[l], gdn_norm_w[l], conv_sc_w[l],
                   w_out[l], norm_mix_post[l], norm_mlp_pre[l], w_up[l], w_down[l], norm_mlp_post[l],
                   bsz=bsz, seq=seq)
    return h.reshape(bsz, seq, d)
```

```python
import functools

import jax
import jax.numpy as jnp
from jax import lax
from jax.experimental import pallas as pl
from jax.experimental.pallas import tpu as pltpu

F32 = jnp.float32
BF16 = jnp.bfloat16

D_MODEL = 4096
CHUNK = 64
HEAD_DIM = 128
GDN_WIDTH = 2048
GDN_HEADS = GDN_WIDTH // HEAD_DIM
SC_WIDTH = 2048
GDN_CONV = 4
SC_CONV = 3
NORM_EPS = 1e-6
L2_EPS = 1e-6

V7X_VMEM_LIMIT_BYTES = 56 * 1024 * 1024

NT_DIMS = (((1,), (1,)), ((), ()))


def _params(*sem):
    return pltpu.CompilerParams(dimension_semantics=sem, vmem_limit_bytes=V7X_VMEM_LIMIT_BYTES)


def _sigmoid(x):
    return 1.0 / (1.0 + jnp.exp(-x))


def _rmsnorm_kernel(x_ref, g_ref, o_ref):
    x = x_ref[...]
    ms = jnp.mean(x * x, axis=-1, keepdims=True)
    o_ref[...] = (x * lax.rsqrt(ms + NORM_EPS) * g_ref[...]).astype(o_ref.dtype)


def _rmsnorm(x, g, tm=256):
    m, d = x.shape
    return pl.pallas_call(
        _rmsnorm_kernel,
        grid=(m // tm,),
        in_specs=[pl.BlockSpec((tm, d), lambda i: (i, 0)), pl.BlockSpec((1, d), lambda i: (0, 0))],
        out_specs=pl.BlockSpec((tm, d), lambda i: (i, 0)),
        out_shape=jax.ShapeDtypeStruct((m, d), BF16),
        compiler_params=_params("parallel"),
        name="rmsnorm",
    )(x, g.reshape(1, d))


def _mm_kernel(a_ref, w_ref, o_ref, *, nk, act, w_rows_are_outputs):
    def part():
        w = w_ref[...].astype(BF16)
        if w_rows_are_outputs:
            return lax.dot_general(a_ref[...], w, NT_DIMS, preferred_element_type=F32)
        return jnp.dot(a_ref[...], w, preferred_element_type=F32)

    if nk == 1:
        acc = part()
        if act == "relu2":
            r = jnp.maximum(acc, 0.0)
            acc = r * r
        o_ref[...] = acc.astype(o_ref.dtype)
        return
    k = pl.program_id(2)

    @pl.when(k == 0)
    def _():
        o_ref[...] = part()

    @pl.when(k > 0)
    def _():
        o_ref[...] += part()


def _a_buffering(nk):
    return pl.Buffered(1) if nk == 1 else None


def _matmul(a, w, *, tm, tn, tk, out_dtype, n_out=None, act=None, w_rows_are_outputs=False, name="matmul"):
    m, kdim = a.shape
    if n_out is None:
        n_out = w.shape[0] if w_rows_are_outputs else w.shape[1]
    n = n_out
    nk = kdim // tk
    assert nk == 1 or (act is None and out_dtype == F32)
    if w_rows_are_outputs:
        w_spec = pl.BlockSpec((tn, tk), lambda i, j, k: (j, k))
    else:
        w_spec = pl.BlockSpec((tk, tn), lambda i, j, k: (k, j))
    return pl.pallas_call(
        functools.partial(_mm_kernel, nk=nk, act=act, w_rows_are_outputs=w_rows_are_outputs),
        grid=(m // tm, n // tn, nk),
        in_specs=[pl.BlockSpec((tm, tk), lambda i, j, k: (i, k), pipeline_mode=_a_buffering(nk)), w_spec],
        out_specs=pl.BlockSpec((tm, tn), lambda i, j, k: (i, j)),
        out_shape=jax.ShapeDtypeStruct((m, n), out_dtype),
        compiler_params=_params("parallel", "parallel", "arbitrary"),
        name=name,
    )(a, w)


def _mm_pair_kernel(a1_ref, a2_ref, w1_ref, w2_ref, o_ref):
    acc = jnp.dot(a1_ref[...], w1_ref[...].astype(BF16), preferred_element_type=F32)
    acc = acc + jnp.dot(a2_ref[...], w2_ref[...].astype(BF16), preferred_element_type=F32)
    o_ref[...] = acc.astype(o_ref.dtype)


def _matmul_pair(a1, a2, w, *, tm, tn, out_dtype, name):
    m, k1 = a1.shape
    _, k2 = a2.shape
    assert k1 == k2 and w.shape[0] == k1 + k2
    n = w.shape[1]
    return pl.pallas_call(
        _mm_pair_kernel,
        grid=(m // tm, n // tn),
        in_specs=[pl.BlockSpec((tm, k1), lambda i, j: (i, 0), pipeline_mode=pl.Buffered(1)),
                  pl.BlockSpec((tm, k2), lambda i, j: (i, 0), pipeline_mode=pl.Buffered(1)),
                  pl.BlockSpec((k1, tn), lambda i, j: (0, j)), pl.BlockSpec((k2, tn), lambda i, j: (1, j))],
        out_specs=pl.BlockSpec((tm, tn), lambda i, j: (i, j)),
        out_shape=jax.ShapeDtypeStruct((m, n), out_dtype),
        compiler_params=_params("parallel", "parallel"),
        name=name,
    )(a1, a2, w, w)


def _gate_kernel(a_ref, w_ref, alog_ref, dtb_ref, g_ref, beta_ref):
    p = lax.dot_general(a_ref[...], w_ref[...].astype(BF16), NT_DIMS, preferred_element_type=F32)
    a = p + dtb_ref[...]
    softplus = jnp.maximum(a, 0.0) + jnp.log(1.0 + jnp.exp(-jnp.abs(a)))
    g_ref[...] = -jnp.exp(alog_ref[...]) * softplus
    beta_ref[...] = _sigmoid(pltpu.roll(p, 128 - GDN_HEADS, axis=1))


def _gate_proj(xn, w_t, a_log, dt_bias, *, row0, tm=1024):
    m, d = xn.shape
    row = lambda i: (i, 0)
    fixed = lambda i: (0, 0)
    return pl.pallas_call(
        _gate_kernel,
        grid=(m // tm,),
        in_specs=[pl.BlockSpec((tm, d), row), pl.BlockSpec((128, d), lambda i: (row0 // 128, 0)),
                  pl.BlockSpec((1, 128), fixed), pl.BlockSpec((1, 128), fixed)],
        out_specs=[pl.BlockSpec((tm, 128), row), pl.BlockSpec((tm, 128), row)],
        out_shape=[jax.ShapeDtypeStruct((m, 128), F32), jax.ShapeDtypeStruct((m, 128), F32)],
        compiler_params=_params("parallel"),
        name="gate_proj",
    )(xn, w_t, a_log, dt_bias)


def _gdn_kernel(q_ref, k_ref, v_ref, z_ref, qh_ref, kh_ref, vh_ref, g_ref, b_ref, wq_ref, wk_ref, wv_ref, nw_ref,
                o_ref, s_ref, u_ref, wqd_ref, kdt_ref, qkm_ref, egl_ref, *, hb, nchunk):
    head0 = pl.program_id(1) * hb
    tile = pl.program_id(2)

    @pl.when(tile == 0)
    def _():
        s_ref[...] = jnp.zeros_like(s_ref)

    row = lax.broadcasted_iota(jnp.int32, (CHUNK, CHUNK), 0)
    col = lax.broadcasted_iota(jnp.int32, (CHUNK, CHUNK), 1)
    tri_incl = row >= col
    tri_strict = row > col
    tri_f = tri_incl.astype(F32)
    eye = (row == col).astype(F32)
    shift = (128 - head0) % 128
    nt = (((1,), (1,)), ((), ()))
    keep_halo = jnp.where(tile > 0, 1.0, 0.0)

    def precompute(c, carry):
        r0 = pl.multiple_of(c * CHUNK, CHUNK)
        rp = pl.multiple_of(jnp.maximum(r0 - 8, 0), 8)
        first = c == 0
        rows = pl.ds(r0, CHUNK)

        graw = pltpu.roll(g_ref[0, rows, :], shift, axis=1)
        beta = pltpu.roll(b_ref[0, rows, :], shift, axis=1)
        gc = jnp.dot(tri_f, graw, preferred_element_type=F32, precision=lax.Precision.HIGHEST)
        eg = jnp.exp(gc)
        g_last = gc[CHUNK - 1:CHUNK, :]
        eg_rest = jnp.exp(g_last - gc)
        egl_ref[c] = jnp.broadcast_to(jnp.exp(g_last), (8, 128))
        gc_t = gc.T

        def conv_silu(ref, halo_ref, w_ref, lanes):
            cur = ref[0, rows, lanes]
            prev = jnp.where(first, halo_ref[0, :, lanes] * keep_halo, ref[0, pl.ds(rp, 8), lanes])
            ext = jnp.concatenate([prev, cur], axis=0)
            w = w_ref[:, lanes]
            acc = cur * w[GDN_CONV - 1:GDN_CONV, :]
            for t in range(GDN_CONV - 1):
                off = 8 - (GDN_CONV - 1) + t
                acc = acc + ext[off:off + CHUNK, :] * w[t:t + 1, :]
            return acc * _sigmoid(acc)

        heads = range(hb)
        lanes = [slice(j * HEAD_DIM, (j + 1) * HEAD_DIM) for j in heads]
        q = [conv_silu(q_ref, qh_ref, wq_ref, lanes[j]) for j in heads]
        k = [conv_silu(k_ref, kh_ref, wk_ref, lanes[j]) for j in heads]
        v = [conv_silu(v_ref, vh_ref, wv_ref, lanes[j]) for j in heads]
        q = [x * (lax.rsqrt(jnp.sum(x * x, axis=-1, keepdims=True) + L2_EPS) * (HEAD_DIM ** -0.5)) for x in q]
        k = [x * lax.rsqrt(jnp.sum(x * x, axis=-1, keepdims=True) + L2_EPS) for x in k]
        kb = [x.astype(BF16) for x in k]
        kk = [lax.dot_general(kb[j], kb[j], nt, preferred_element_type=F32) for j in heads]
        qk = [lax.dot_general(q[j].astype(BF16), kb[j], nt, preferred_element_type=F32) for j in heads]

        beta_col = [beta[:, j:j + 1] for j in heads]
        decay = [jnp.where(tri_incl, jnp.exp(gc[:, j:j + 1] - gc_t[j:j + 1, :]), 0.0) for j in heads]
        low = [jnp.where(tri_strict, beta_col[j] * kk[j] * decay[j], 0.0) for j in heads]

        inv = [eye - x for x in low]
        pw = low
        for _ in range(5):
            pwb = [x.astype(BF16) for x in pw]
            pw = [jnp.dot(x, x, preferred_element_type=F32) for x in pwb]
            inv = [inv[j] + jnp.dot(inv[j].astype(BF16), pw[j].astype(BF16), preferred_element_type=F32)
                   for j in heads]

        rhs = [jnp.concatenate([v[j] * beta_col[j], k[j] * (beta_col[j] * eg[:, j:j + 1])], axis=1) for j in heads]
        sol = [jnp.dot(inv[j].astype(BF16), rhs[j].astype(BF16), preferred_element_type=F32) for j in heads]
        for j in heads:
            u_ref[j, rows, :] = sol[j][:, :HEAD_DIM]
            wqd_ref[j, c, :CHUNK, :] = sol[j][:, HEAD_DIM:].astype(BF16)
            wqd_ref[j, c, CHUNK:, :] = (q[j] * eg[:, j:j + 1]).astype(BF16)
            kdt_ref[j, c] = (k[j] * eg_rest[:, j:j + 1]).T.astype(BF16)
            qkm_ref[j, rows, :] = (qk[j] * decay[j]).astype(BF16)
        return carry

    def scan(c, carry):
        r0 = pl.multiple_of(c * CHUNK, CHUNK)
        rows = pl.ds(r0, CHUNK)
        egl = egl_ref[c]
        heads = range(hb)
        state = [s_ref[j] for j in heads]
        proj = [jnp.dot(wqd_ref[j, c], state[j].astype(BF16), preferred_element_type=F32) for j in heads]
        v_new = [(u_ref[j, rows, :] - proj[j][:CHUNK]).astype(BF16) for j in heads]
        for j in heads:
            s_ref[j] = state[j] * egl[0:1, j:j + 1] + jnp.dot(kdt_ref[j, c], v_new[j], preferred_element_type=F32)
        o = [proj[j][CHUNK:] + jnp.dot(qkm_ref[j, rows, :], v_new[j], preferred_element_type=F32) for j in heads]
        for j in heads:
            lanes = slice(j * HEAD_DIM, (j + 1) * HEAD_DIM)
            ms = jnp.mean(o[j] * o[j], axis=-1, keepdims=True)
            z = z_ref[0, rows, lanes]
            out = o[j] * lax.rsqrt(ms + NORM_EPS) * nw_ref[...] * (z * _sigmoid(z))
            o_ref[0, rows, lanes] = out.astype(o_ref.dtype)
        return carry

    lax.fori_loop(0, nchunk, precompute, 0)
    lax.fori_loop(0, nchunk, scan, 0)


def _gdn(proj, g, beta, conv_w, norm_w, *, bsz, seq, hb=8, ts=512):
    nblk = GDN_WIDTH // (HEAD_DIM * hb)
    wblk = HEAD_DIM * hb
    nchunk = ts // CHUNK

    def col(group):
        return pl.BlockSpec((1, ts, wblk), lambda b, h, t: (b, t, group * nblk + h))

    def halo(group):
        return pl.BlockSpec((1, 8, wblk), lambda b, h, t: (b, jnp.maximum(t * (ts // 8) - 1, 0), group * nblk + h))

    def wcol(group):
        return pl.BlockSpec((GDN_CONV, wblk), lambda b, h, t: (0, group * nblk + h))

    gspec = pl.BlockSpec((1, ts, 128), lambda b, h, t: (b, t, 0))
    return pl.pallas_call(
        functools.partial(_gdn_kernel, hb=hb, nchunk=nchunk),
        grid=(bsz, nblk, seq // ts),
        in_specs=[col(0), col(1), col(2), col(3), halo(0), halo(1), halo(2), gspec, gspec,
                  wcol(0), wcol(1), wcol(2), pl.BlockSpec((1, HEAD_DIM), lambda b, h, t: (0, 0))],
        out_specs=pl.BlockSpec((1, ts, wblk), lambda b, h, t: (b, t, h)),
        out_shape=jax.ShapeDtypeStruct((bsz, seq, GDN_WIDTH), BF16),
        scratch_shapes=[pltpu.VMEM((hb, HEAD_DIM, HEAD_DIM), F32),
                        pltpu.VMEM((hb, ts, HEAD_DIM), F32),
                        pltpu.VMEM((hb, nchunk, 2 * CHUNK, HEAD_DIM), BF16),
                        pltpu.VMEM((hb, nchunk, HEAD_DIM, CHUNK), BF16),
                        pltpu.VMEM((hb, ts, CHUNK), BF16),
                        pltpu.VMEM((nchunk, 8, 128), F32)],
        compiler_params=_params("parallel", "parallel", "arbitrary"),
        name="gdn",
    )(proj, proj, proj, proj, proj, proj, proj, g, beta, conv_w, conv_w, conv_w, norm_w)


def _sc_kernel(a_ref, wb0_ref, wb1_ref, wc0_ref, wc1_ref, wh0_ref, wh1_ref, cw_ref, o_ref, p_ref, b_ref, *,
               row_off, rows_per_step, nstep):
    def proj(w0_ref, w1_ref):
        w_t = jnp.concatenate([w0_ref[row_off:, :], w1_ref[...]], axis=0)
        return lax.dot_general(a_ref[...], w_t, NT_DIMS, preferred_element_type=F32)

    p_ref[0:8, :] = jnp.zeros((8, p_ref.shape[1]), F32)
    p_ref[8:, :] = proj(wc0_ref, wc1_ref)
    p_ref[8:, :] *= proj(wh0_ref, wh1_ref)
    b_ref[...] = proj(wb0_ref, wb1_ref)
    w = cw_ref[...]

    def step(i, carry):
        r0 = pl.multiple_of(i * rows_per_step, rows_per_step)
        ext = p_ref[pl.ds(r0, rows_per_step + 8), :]
        acc = ext[8:, :] * w[SC_CONV - 1:SC_CONV, :]
        for t in range(SC_CONV - 1):
            off = 8 - (SC_CONV - 1) + t
            acc = acc + ext[off:off + rows_per_step, :] * w[t:t + 1, :]
        rows = pl.ds(r0, rows_per_step)
        o_ref[rows, :] = (b_ref[rows, :] * acc).astype(o_ref.dtype)
        return carry

    lax.fori_loop(0, nstep, step, 0)


def _sc_proj_conv(xn, w_ext_t, conv_w, *, seq, row_off, tc=256, rows_per_step=128):
    m, d = xn.shape
    nblk = SC_WIDTH // tc
    sub = tc // row_off

    def wrows(group):
        lead = pl.BlockSpec((tc, d), lambda b, j: (group * nblk + j, 0))
        tail = pl.BlockSpec((row_off, d), lambda b, j: ((group * nblk + j + 1) * sub, 0))
        return [lead, tail]

    return pl.pallas_call(
        functools.partial(_sc_kernel, row_off=row_off, rows_per_step=rows_per_step, nstep=seq // rows_per_step),
        grid=(m // seq, nblk),
        in_specs=[pl.BlockSpec((seq, d), lambda b, j: (b, 0), pipeline_mode=pl.Buffered(1)),
                  *wrows(0), *wrows(1), *wrows(2), pl.BlockSpec((SC_CONV, tc), lambda b, j: (0, j))],
        out_specs=pl.BlockSpec((seq, tc), lambda b, j: (b, j)),
        out_shape=jax.ShapeDtypeStruct((m, SC_WIDTH), BF16),
        scratch_shapes=[pltpu.VMEM((8 + seq, tc), F32), pltpu.VMEM((seq, tc), F32)],
        compiler_params=_params("parallel", "parallel"),
        name="sc_proj_conv",
    )(xn, *([w_ext_t] * 6), conv_w)


def _res_norm_kernel(t_ref, r_ref, gpost_ref, gnext_ref, h_ref, hn_ref):
    t = t_ref[...]
    h = r_ref[...] + t * lax.rsqrt(jnp.mean(t * t, axis=-1, keepdims=True) + NORM_EPS) * gpost_ref[...]
    h_ref[...] = h
    hn = h * lax.rsqrt(jnp.mean(h * h, axis=-1, keepdims=True) + NORM_EPS) * gnext_ref[...]
    hn_ref[...] = hn.astype(hn_ref.dtype)


def _res_norm(t, resid, g_post, g_next, tm=256):
    m, d = t.shape
    row = lambda i: (i, 0)
    fixed = lambda i: (0, 0)
    return pl.pallas_call(
        _res_norm_kernel,
        grid=(m // tm,),
        in_specs=[pl.BlockSpec((tm, d), row), pl.BlockSpec((tm, d), row),
                  pl.BlockSpec((1, d), fixed), pl.BlockSpec((1, d), fixed)],
        out_specs=[pl.BlockSpec((tm, d), row), pl.BlockSpec((tm, d), row)],
        out_shape=[jax.ShapeDtypeStruct((m, d), F32), jax.ShapeDtypeStruct((m, d), BF16)],
        compiler_params=_params("parallel"),
        name="res_norm",
    )(t, resid, g_post.reshape(1, d), g_next.reshape(1, d))


def _res_final_kernel(t_ref, r_ref, g_ref, o_ref):
    t = t_ref[...]
    o_ref[...] = r_ref[...] + t * lax.rsqrt(jnp.mean(t * t, axis=-1, keepdims=True) + NORM_EPS) * g_ref[...]


def _res_final(t, resid, g, tm=256):
    m, d = t.shape
    row = lambda i: (i, 0)
    return pl.pallas_call(
        _res_final_kernel,
        grid=(m // tm,),
        in_specs=[pl.BlockSpec((tm, d), row), pl.BlockSpec((tm, d), row), pl.BlockSpec((1, d), lambda i: (0, 0))],
        out_specs=pl.BlockSpec((tm, d), row),
        out_shape=jax.ShapeDtypeStruct((m, d), F32),
        compiler_params=_params("parallel"),
        name="res_final",
    )(t, resid, g.reshape(1, d))


def _layer(h, norm_mix_pre, w_in, conv_qkv_w, a_log, dt_bias, gdn_norm_w, conv_sc_w, w_out,
           norm_mix_post, norm_mlp_pre, w_up, w_down, norm_mlp_post, *, bsz, seq):
    m = bsz * seq
    n_gdn = 4 * GDN_WIDTH
    n_ab = 2 * GDN_HEADS

    xn = _rmsnorm(h, norm_mix_pre)

    w_in_t = w_in.T
    w_ext_t = w_in_t[n_gdn:].astype(BF16)
    lane_pad = jnp.zeros((1, 128 - GDN_HEADS), F32)
    a_log_p = jnp.concatenate([a_log.reshape(1, -1), lane_pad], axis=1)
    dt_bias_p = jnp.concatenate([dt_bias.reshape(1, -1), lane_pad], axis=1)

    proj = _matmul(xn, w_in_t, tm=2048, tn=512, tk=D_MODEL, out_dtype=F32, n_out=n_gdn, w_rows_are_outputs=True,
                   name="in_proj")
    g, beta = _gate_proj(xn, w_in_t, a_log_p, dt_bias_p, row0=n_gdn)

    proj3 = proj.reshape(bsz, seq, -1)
    gdn_out = _gdn(proj3, g.reshape(bsz, seq, 128), beta.reshape(bsz, seq, 128), conv_qkv_w,
                   gdn_norm_w.reshape(1, HEAD_DIM), bsz=bsz, seq=seq)
    sc_out = _sc_proj_conv(xn, w_ext_t, conv_sc_w, seq=seq, row_off=n_ab)

    mix = _matmul_pair(gdn_out.reshape(m, -1), sc_out, w_out, tm=2048, tn=512,
                       out_dtype=F32, name="out_proj")
    h, hn = _res_norm(mix, h, norm_mix_post, norm_mlp_pre)

    hid = _matmul(hn, w_up, tm=2048, tn=512, tk=D_MODEL, out_dtype=BF16, act="relu2", name="mlp_up")
    ff = _matmul(hid, w_down, tm=2048, tn=1024, tk=1024, out_dtype=F32, name="mlp_down")
    return _res_final(ff, h, norm_mlp_post)


def kernel(x, norm_mix_pre, w_in, conv_qkv_w, a_log, dt_bias, gdn_norm_w, conv_sc_w, w_out, norm_mix_post,
           norm_mlp_pre, w_up, w_down, norm_mlp_post):
    bsz, seq, d = x.shape
    h = x.reshape(bsz * seq, d)
    for l in range(norm_mix_pre.shape[0]):
        h = _layer(h, norm_mix_pre[l], w_in[l], conv_qkv_w[l], a_log[l], dt_bias[l], gdn_norm_w[l], conv_sc_w[l],
                   w_out[l], norm_mix_post[l], norm_mlp_pre[l], w_up[l], w_down[l], norm_mlp_post[l],
                   bsz=bsz, seq=seq)
    return h.reshape(bsz, seq, d)
```

```python
import functools

import jax
import jax.numpy as jnp
from jax import lax
from jax.experimental import pallas as pl
from jax.experimental.pallas import tpu as pltpu

F32 = jnp.float32
BF16 = jnp.bfloat16

D_MODEL = 4096
CHUNK = 64
HEAD_DIM = 128
GDN_WIDTH = 2048
GDN_HEADS = GDN_WIDTH // HEAD_DIM
SC_WIDTH = 2048
GDN_CONV = 4
SC_CONV = 3
NORM_EPS = 1e-6
L2_EPS = 1e-6

V7X_VMEM_LIMIT_BYTES = 60 * 1024 * 1024

NT_DIMS = (((1,), (1,)), ((), ()))


def _params(*sem):
    return pltpu.CompilerParams(dimension_semantics=sem, vmem_limit_bytes=V7X_VMEM_LIMIT_BYTES)


def _sigmoid(x):
    return 1.0 / (1.0 + jnp.exp(-x))


def _rmsnorm_kernel(x_ref, g_ref, o_ref):
    x = x_ref[...]
    ms = jnp.mean(x * x, axis=-1, keepdims=True)
    o_ref[...] = (x * lax.rsqrt(ms + NORM_EPS) * g_ref[...]).astype(o_ref.dtype)


def _rmsnorm(x, g, tm=256):
    m, d = x.shape
    return pl.pallas_call(
        _rmsnorm_kernel,
        grid=(m // tm,),
        in_specs=[pl.BlockSpec((tm, d), lambda i: (i, 0)), pl.BlockSpec((1, d), lambda i: (0, 0))],
        out_specs=pl.BlockSpec((tm, d), lambda i: (i, 0)),
        out_shape=jax.ShapeDtypeStruct((m, d), BF16),
        compiler_params=_params("parallel"),
        name="rmsnorm",
    )(x, g.reshape(1, d))


def _mm_kernel(a_ref, w_ref, o_ref, *, nk, act, w_rows_are_outputs):
    def part():
        w = w_ref[...].astype(BF16)
        if w_rows_are_outputs:
            return lax.dot_general(a_ref[...], w, NT_DIMS, preferred_element_type=F32)
        return jnp.dot(a_ref[...], w, preferred_element_type=F32)

    if nk == 1:
        acc = part()
        if act == "relu2":
            r = jnp.maximum(acc, 0.0)
            acc = r * r
        o_ref[...] = acc.astype(o_ref.dtype)
        return
    k = pl.program_id(2)

    @pl.when(k == 0)
    def _():
        o_ref[...] = part()

    @pl.when(k > 0)
    def _():
        o_ref[...] += part()


def _a_buffering(nk):
    return pl.Buffered(1) if nk == 1 else None


def _matmul(a, w, *, tm, tn, tk, out_dtype, n_out=None, act=None, w_rows_are_outputs=False, name="matmul"):
    m, kdim = a.shape
    if n_out is None:
        n_out = w.shape[0] if w_rows_are_outputs else w.shape[1]
    n = n_out
    nk = kdim // tk
    assert nk == 1 or (act is None and out_dtype == F32)
    if w_rows_are_outputs:
        w_spec = pl.BlockSpec((tn, tk), lambda i, j, k: (j, k))
    else:
        w_spec = pl.BlockSpec((tk, tn), lambda i, j, k: (k, j))
    return pl.pallas_call(
        functools.partial(_mm_kernel, nk=nk, act=act, w_rows_are_outputs=w_rows_are_outputs),
        grid=(m // tm, n // tn, nk),
        in_specs=[pl.BlockSpec((tm, tk), lambda i, j, k: (i, k), pipeline_mode=_a_buffering(nk)), w_spec],
        out_specs=pl.BlockSpec((tm, tn), lambda i, j, k: (i, j)),
        out_shape=jax.ShapeDtypeStruct((m, n), out_dtype),
        compiler_params=_params("parallel", "parallel", "arbitrary"),
        name=name,
    )(a, w)


def _mm_pair_kernel(a1_ref, a2_ref, w1_ref, w2_ref, o_ref):
    acc = jnp.dot(a1_ref[...], w1_ref[...].astype(BF16), preferred_element_type=F32)
    acc = acc + jnp.dot(a2_ref[...], w2_ref[...].astype(BF16), preferred_element_type=F32)
    o_ref[...] = acc.astype(o_ref.dtype)


def _matmul_pair(a1, a2, w, *, tm, tn, out_dtype, name):
    m, k1 = a1.shape
    _, k2 = a2.shape
    assert k1 == k2 and w.shape[0] == k1 + k2
    n = w.shape[1]
    return pl.pallas_call(
        _mm_pair_kernel,
        grid=(m // tm, n // tn),
        in_specs=[pl.BlockSpec((tm, k1), lambda i, j: (i, 0), pipeline_mode=pl.Buffered(1)),
                  pl.BlockSpec((tm, k2), lambda i, j: (i, 0), pipeline_mode=pl.Buffered(1)),
                  pl.BlockSpec((k1, tn), lambda i, j: (0, j)), pl.BlockSpec((k2, tn), lambda i, j: (1, j))],
        out_specs=pl.BlockSpec((tm, tn), lambda i, j: (i, j)),
        out_shape=jax.ShapeDtypeStruct((m, n), out_dtype),
        compiler_params=_params("parallel", "parallel"),
        name=name,
    )(a1, a2, w, w)


def _gate_kernel(a_ref, w_ref, alog_ref, dtb_ref, g_ref, beta_ref):
    p = lax.dot_general(a_ref[...], w_ref[...].astype(BF16), NT_DIMS, preferred_element_type=F32)
    a = p + dtb_ref[...]
    softplus = jnp.maximum(a, 0.0) + jnp.log(1.0 + jnp.exp(-jnp.abs(a)))
    g_ref[...] = -jnp.exp(alog_ref[...]) * softplus
    beta_ref[...] = _sigmoid(pltpu.roll(p, 128 - GDN_HEADS, axis=1))


def _gate_proj(xn, w_t, a_log, dt_bias, *, row0, tm=1024):
    m, d = xn.shape
    row = lambda i: (i, 0)
    fixed = lambda i: (0, 0)
    return pl.pallas_call(
        _gate_kernel,
        grid=(m // tm,),
        in_specs=[pl.BlockSpec((tm, d), row), pl.BlockSpec((128, d), lambda i: (row0 // 128, 0)),
                  pl.BlockSpec((1, 128), fixed), pl.BlockSpec((1, 128), fixed)],
        out_specs=[pl.BlockSpec((tm, 128), row), pl.BlockSpec((tm, 128), row)],
        out_shape=[jax.ShapeDtypeStruct((m, 128), F32), jax.ShapeDtypeStruct((m, 128), F32)],
        compiler_params=_params("parallel"),
        name="gate_proj",
    )(xn, w_t, a_log, dt_bias)


def _gdn_kernel(q_ref, k_ref, v_ref, z_ref, qh_ref, kh_ref, vh_ref, g_ref, b_ref, wq_ref, wk_ref, wv_ref, nw_ref,
                o_ref, s_ref, u_ref, wqd_ref, kdt_ref, qkm_ref, egl_ref, *, hb, nchunk):
    head0 = pl.program_id(1) * hb
    tile = pl.program_id(2)

    @pl.when(tile == 0)
    def _():
        s_ref[...] = jnp.zeros_like(s_ref)

    row = lax.broadcasted_iota(jnp.int32, (CHUNK, CHUNK), 0)
    col = lax.broadcasted_iota(jnp.int32, (CHUNK, CHUNK), 1)
    tri_incl = row >= col
    tri_strict = row > col
    tri_f = tri_incl.astype(F32)
    eye = (row == col).astype(F32)
    shift = (128 - head0) % 128
    nt = (((1,), (1,)), ((), ()))
    keep_halo = jnp.where(tile > 0, 1.0, 0.0)

    heads = range(hb)
    head_lanes = [slice(j * HEAD_DIM, (j + 1) * HEAD_DIM) for j in heads]

    def precompute(c, first_in_tile=False):
        r0 = pl.multiple_of(c * CHUNK, CHUNK)
        rows = pl.ds(r0, CHUNK)

        graw = pltpu.roll(g_ref[0, rows, :], shift, axis=1)
        beta = pltpu.roll(b_ref[0, rows, :], shift, axis=1)
        gc = jnp.dot(tri_f, graw, preferred_element_type=F32, precision=lax.Precision.HIGHEST)
        eg = jnp.exp(gc)
        g_last = gc[CHUNK - 1:CHUNK, :]
        eg_rest = jnp.exp(g_last - gc)
        gc_t = gc.T

        def conv_silu(ref, halo_ref, w_ref, lanes):
            cur = ref[0, rows, lanes]
            w = w_ref[:, lanes]
            acc = cur * w[GDN_CONV - 1:GDN_CONV, :]
            if first_in_tile:
                prev = halo_ref[0, :, lanes] * keep_halo
            else:
                prev = ref[0, pl.ds(pl.multiple_of(r0 - 8, 8), 8), lanes]
            ext = jnp.concatenate([prev, cur], axis=0)
            for t in range(GDN_CONV - 1):
                back = GDN_CONV - 1 - t
                acc = acc + ext[8 - back:8 - back + CHUNK, :] * w[t:t + 1, :]
            return acc * _sigmoid(acc)

        q = [conv_silu(q_ref, qh_ref, wq_ref, head_lanes[j]) for j in heads]
        k = [conv_silu(k_ref, kh_ref, wk_ref, head_lanes[j]) for j in heads]
        v = [conv_silu(v_ref, vh_ref, wv_ref, head_lanes[j]) for j in heads]
        q = [x * (lax.rsqrt(jnp.sum(x * x, axis=-1, keepdims=True) + L2_EPS) * (HEAD_DIM ** -0.5)) for x in q]
        k = [x * lax.rsqrt(jnp.sum(x * x, axis=-1, keepdims=True) + L2_EPS) for x in k]
        kb = [x.astype(BF16) for x in k]
        kk = [lax.dot_general(kb[j], kb[j], nt, preferred_element_type=F32) for j in heads]
        qk = [lax.dot_general(q[j].astype(BF16), kb[j], nt, preferred_element_type=F32) for j in heads]
        yield

        beta_col = [beta[:, j:j + 1] for j in heads]
        decay = [jnp.where(tri_incl, jnp.exp(gc[:, j:j + 1] - gc_t[j:j + 1, :]), 0.0) for j in heads]
        low = [jnp.where(tri_strict, beta_col[j] * kk[j] * decay[j], 0.0) for j in heads]

        inv = [eye - x for x in low]
        pw = low
        for _ in range(5):
            pwb = [x.astype(BF16) for x in pw]
            pw = [jnp.dot(x, x, preferred_element_type=F32) for x in pwb]
            inv = [inv[j] + jnp.dot(inv[j].astype(BF16), pw[j].astype(BF16), preferred_element_type=F32)
                   for j in heads]
            yield

        rhs = [jnp.concatenate([v[j] * beta_col[j], k[j] * (beta_col[j] * eg[:, j:j + 1])], axis=1) for j in heads]
        sol = [jnp.dot(inv[j].astype(BF16), rhs[j].astype(BF16), preferred_element_type=F32) for j in heads]
        yield
        egl_ref[c] = jnp.broadcast_to(jnp.exp(g_last), (8, 128))
        for j in heads:
            u_ref[j, rows, :] = sol[j][:, :HEAD_DIM]
            wqd_ref[j, c, :CHUNK, :] = sol[j][:, HEAD_DIM:].astype(BF16)
            wqd_ref[j, c, CHUNK:, :] = (q[j] * eg[:, j:j + 1]).astype(BF16)
            kdt_ref[j, c] = (k[j] * eg_rest[:, j:j + 1]).T.astype(BF16)
            qkm_ref[j, rows, :] = (qk[j] * decay[j]).astype(BF16)

    def scan(c):
        r0 = pl.multiple_of(c * CHUNK, CHUNK)
        rows = pl.ds(r0, CHUNK)
        egl = egl_ref[c]
        state = [s_ref[j] for j in heads]
        u = [u_ref[j, rows, :] for j in heads]
        kdt = [kdt_ref[j, c] for j in heads]
        qkm = [qkm_ref[j, rows, :] for j in heads]
        proj = [jnp.dot(wqd_ref[j, c], state[j].astype(BF16), preferred_element_type=F32) for j in heads]
        yield
        v_new = [(u[j] - proj[j][:CHUNK]).astype(BF16) for j in heads]
        for j in heads:
            s_ref[j] = state[j] * egl[0:1, j:j + 1] + jnp.dot(kdt[j], v_new[j], preferred_element_type=F32)
        o = [proj[j][CHUNK:] + jnp.dot(qkm[j], v_new[j], preferred_element_type=F32) for j in heads]
        yield
        for j in heads:
            ms = jnp.mean(o[j] * o[j], axis=-1, keepdims=True)
            z = z_ref[0, rows, head_lanes[j]]
            out = o[j] * lax.rsqrt(ms + NORM_EPS) * nw_ref[...] * (z * _sigmoid(z))
            o_ref[0, rows, head_lanes[j]] = out.astype(o_ref.dtype)

    def interleave(*parts):
        parts = list(parts)
        while parts:
            parts = [p for p in parts if next(p, StopIteration) is not StopIteration]

    def chain(*parts):
        for p in parts:
            yield from p

    assert nchunk % 2 == 0
    interleave(precompute(0, first_in_tile=True), precompute(1))

    def steady(i, carry):
        c = 2 * i
        interleave(precompute(c + 2), precompute(c + 3), chain(scan(c), scan(c + 1)))
        return carry

    lax.fori_loop(0, nchunk // 2 - 1, steady, 0)
    interleave(chain(scan(nchunk - 2), scan(nchunk - 1)))


def _gdn(proj, g, beta, conv_w, norm_w, *, bsz, seq, hb=8, ts=512):
    nblk = GDN_WIDTH // (HEAD_DIM * hb)
    wblk = HEAD_DIM * hb
    nchunk = ts // CHUNK

    def col(group):
        return pl.BlockSpec((1, ts, wblk), lambda b, h, t: (b, t, group * nblk + h))

    def halo(group):
        return pl.BlockSpec((1, 8, wblk), lambda b, h, t: (b, jnp.maximum(t * (ts // 8) - 1, 0), group * nblk + h))

    def wcol(group):
        return pl.BlockSpec((GDN_CONV, wblk), lambda b, h, t: (0, group * nblk + h))

    gspec = pl.BlockSpec((1, ts, 128), lambda b, h, t: (b, t, 0))
    return pl.pallas_call(
        functools.partial(_gdn_kernel, hb=hb, nchunk=nchunk),
        grid=(bsz, nblk, seq // ts),
        in_specs=[col(0), col(1), col(2), col(3), halo(0), halo(1), halo(2), gspec, gspec,
                  wcol(0), wcol(1), wcol(2), pl.BlockSpec((1, HEAD_DIM), lambda b, h, t: (0, 0))],
        out_specs=pl.BlockSpec((1, ts, wblk), lambda b, h, t: (b, t, h)),
        out_shape=jax.ShapeDtypeStruct((bsz, seq, GDN_WIDTH), BF16),
        scratch_shapes=[pltpu.VMEM((hb, HEAD_DIM, HEAD_DIM), F32),
                        pltpu.VMEM((hb, ts, HEAD_DIM), F32),
                        pltpu.VMEM((hb, nchunk, 2 * CHUNK, HEAD_DIM), BF16),
                        pltpu.VMEM((hb, nchunk, HEAD_DIM, CHUNK), BF16),
                        pltpu.VMEM((hb, ts, CHUNK), BF16),
                        pltpu.VMEM((nchunk, 8, 128), F32)],
        compiler_params=_params("parallel", "parallel", "arbitrary"),
        name="gdn",
    )(proj, proj, proj, proj, proj, proj, proj, g, beta, conv_w, conv_w, conv_w, norm_w)


def _sc_kernel(a_ref, wb0_ref, wb1_ref, wc0_ref, wc1_ref, wh0_ref, wh1_ref, cw_ref, o_ref, p_ref, b_ref, *,
               row_off, rows_per_step, nstep):
    def proj(w0_ref, w1_ref):
        w_t = jnp.concatenate([w0_ref[row_off:, :], w1_ref[...]], axis=0)
        return lax.dot_general(a_ref[...], w_t, NT_DIMS, preferred_element_type=F32)

    p_ref[0:8, :] = jnp.zeros((8, p_ref.shape[1]), F32)
    p_ref[8:, :] = proj(wc0_ref, wc1_ref)
    p_ref[8:, :] *= proj(wh0_ref, wh1_ref)
    b_ref[...] = proj(wb0_ref, wb1_ref)
    w = cw_ref[...]

    def step(i, carry):
        r0 = pl.multiple_of(i * rows_per_step, rows_per_step)
        ext = p_ref[pl.ds(r0, rows_per_step + 8), :]
        acc = ext[8:, :] * w[SC_CONV - 1:SC_CONV, :]
        for t in range(SC_CONV - 1):
            off = 8 - (SC_CONV - 1) + t
            acc = acc + ext[off:off + rows_per_step, :] * w[t:t + 1, :]
        rows = pl.ds(r0, rows_per_step)
        o_ref[rows, :] = (b_ref[rows, :] * acc).astype(o_ref.dtype)
        return carry

    lax.fori_loop(0, nstep, step, 0)


def _sc_proj_conv(xn, w_ext_t, conv_w, *, seq, row_off, tc=256, rows_per_step=128):
    m, d = xn.shape
    nblk = SC_WIDTH // tc
    sub = tc // row_off

    def wrows(group):
        lead = pl.BlockSpec((tc, d), lambda b, j: (group * nblk + j, 0))
        tail = pl.BlockSpec((row_off, d), lambda b, j: ((group * nblk + j + 1) * sub, 0))
        return [lead, tail]

    return pl.pallas_call(
        functools.partial(_sc_kernel, row_off=row_off, rows_per_step=rows_per_step, nstep=seq // rows_per_step),
        grid=(m // seq, nblk),
        in_specs=[pl.BlockSpec((seq, d), lambda b, j: (b, 0), pipeline_mode=pl.Buffered(1)),
                  *wrows(0), *wrows(1), *wrows(2), pl.BlockSpec((SC_CONV, tc), lambda b, j: (0, j))],
        out_specs=pl.BlockSpec((seq, tc), lambda b, j: (b, j)),
        out_shape=jax.ShapeDtypeStruct((m, SC_WIDTH), BF16),
        scratch_shapes=[pltpu.VMEM((8 + seq, tc), F32), pltpu.VMEM((seq, tc), F32)],
        compiler_params=_params("parallel", "parallel"),
        name="sc_proj_conv",
    )(xn, *([w_ext_t] * 6), conv_w)


def _res_norm_kernel(t_ref, r_ref, gpost_ref, gnext_ref, h_ref, hn_ref):
    t = t_ref[...]
    h = r_ref[...] + t * lax.rsqrt(jnp.mean(t * t, axis=-1, keepdims=True) + NORM_EPS) * gpost_ref[...]
    h_ref[...] = h
    hn = h * lax.rsqrt(jnp.mean(h * h, axis=-1, keepdims=True) + NORM_EPS) * gnext_ref[...]
    hn_ref[...] = hn.astype(hn_ref.dtype)


def _res_norm(t, resid, g_post, g_next, tm=256):
    m, d = t.shape
    row = lambda i: (i, 0)
    fixed = lambda i: (0, 0)
    return pl.pallas_call(
        _res_norm_kernel,
        grid=(m // tm,),
        in_specs=[pl.BlockSpec((tm, d), row), pl.BlockSpec((tm, d), row),
                  pl.BlockSpec((1, d), fixed), pl.BlockSpec((1, d), fixed)],
        out_specs=[pl.BlockSpec((tm, d), row), pl.BlockSpec((tm, d), row)],
        out_shape=[jax.ShapeDtypeStruct((m, d), F32), jax.ShapeDtypeStruct((m, d), BF16)],
        compiler_params=_params("parallel"),
        name="res_norm",
    )(t, resid, g_post.reshape(1, d), g_next.reshape(1, d))


def _res_final_kernel(t_ref, r_ref, g_ref, o_ref):
    t = t_ref[...]
    o_ref[...] = r_ref[...] + t * lax.rsqrt(jnp.mean(t * t, axis=-1, keepdims=True) + NORM_EPS) * g_ref[...]


def _res_final(t, resid, g, tm=256):
    m, d = t.shape
    row = lambda i: (i, 0)
    return pl.pallas_call(
        _res_final_kernel,
        grid=(m // tm,),
        in_specs=[pl.BlockSpec((tm, d), row), pl.BlockSpec((tm, d), row), pl.BlockSpec((1, d), lambda i: (0, 0))],
        out_specs=pl.BlockSpec((tm, d), row),
        out_shape=jax.ShapeDtypeStruct((m, d), F32),
        compiler_params=_params("parallel"),
        name="res_final",
    )(t, resid, g.reshape(1, d))


def _layer(h, norm_mix_pre, w_in, conv_qkv_w, a_log, dt_bias, gdn_norm_w, conv_sc_w, w_out,
           norm_mix_post, norm_mlp_pre, w_up, w_down, norm_mlp_post, *, bsz, seq):
    m = bsz * seq
    n_gdn = 4 * GDN_WIDTH
    n_ab = 2 * GDN_HEADS

    xn = _rmsnorm(h, norm_mix_pre)

    w_in_t = w_in.T
    w_ext_t = w_in_t[n_gdn:].astype(BF16)
    lane_pad = jnp.zeros((1, 128 - GDN_HEADS), F32)
    a_log_p = jnp.concatenate([a_log.reshape(1, -1), lane_pad], axis=1)
    dt_bias_p = jnp.concatenate([dt_bias.reshape(1, -1), lane_pad], axis=1)

    proj = _matmul(xn, w_in_t, tm=2048, tn=512, tk=D_MODEL, out_dtype=F32, n_out=n_gdn, w_rows_are_outputs=True,
                   name="in_proj")
    g, beta = _gate_proj(xn, w_in_t, a_log_p, dt_bias_p, row0=n_gdn)

    proj3 = proj.reshape(bsz, seq, -1)
    gdn_out = _gdn(proj3, g.reshape(bsz, seq, 128), beta.reshape(bsz, seq, 128), conv_qkv_w,
                   gdn_norm_w.reshape(1, HEAD_DIM), bsz=bsz, seq=seq)
    sc_out = _sc_proj_conv(xn, w_ext_t, conv_sc_w, seq=seq, row_off=n_ab)

    mix = _matmul_pair(gdn_out.reshape(m, -1), sc_out, w_out, tm=2048, tn=512,
                       out_dtype=F32, name="out_proj")
    h, hn = _res_norm(mix, h, norm_mix_post, norm_mlp_pre)

    hid = _matmul(hn, w_up, tm=2048, tn=512, tk=D_MODEL, out_dtype=BF16, act="relu2", name="mlp_up")
    ff = _matmul(hid, w_down, tm=2048, tn=1024, tk=2048, out_dtype=F32, name="mlp_down")
    return _res_final(ff, h, norm_mlp_post)


def kernel(x, norm_mix_pre, w_in, conv_qkv_w, a_log, dt_bias, gdn_norm_w, conv_sc_w, w_out, norm_mix_post,
           norm_mlp_pre, w_up, w_down, norm_mlp_post):
    bsz, seq, d = x.shape
    h = x.reshape(bsz * seq, d)
    for l in range(norm_mix_pre.shape[0]):
        h = _layer(h, norm_mix_pre[l], w_in[l], conv_qkv_w[l], a_log[l], dt_bias[l], gdn_norm_w[l], conv_sc_w[l],
                   w_out[l], norm_mix_post[l], norm_mlp_pre[l], w_up[l], w_down[l], norm_mlp_post[l],
                   bsz=bsz, seq=seq)
    return h.reshape(bsz, seq, d)
```

```python
import functools

import jax
import jax.numpy as jnp
from jax import lax
from jax.experimental import pallas as pl
from jax.experimental.pallas import tpu as pltpu

F32 = jnp.float32
BF16 = jnp.bfloat16

D_MODEL = 4096
CHUNK = 64
HEAD_DIM = 128
GDN_WIDTH = 2048
GDN_HEADS = GDN_WIDTH // HEAD_DIM
SC_WIDTH = 2048
GDN_CONV = 4
SC_CONV = 3
NORM_EPS = 1e-6
L2_EPS = 1e-6

V7X_VMEM_LIMIT_BYTES = 60 * 1024 * 1024

NT_DIMS = (((1,), (1,)), ((), ()))


def _params(*sem):
    return pltpu.CompilerParams(dimension_semantics=sem, vmem_limit_bytes=V7X_VMEM_LIMIT_BYTES)


def _sigmoid(x):
    return 1.0 / (1.0 + jnp.exp(-x))


def _silu(x):
    half = 0.5 * x
    return half + half * jnp.tanh(half)


def _rmsnorm_kernel(x_ref, g_ref, o_ref):
    x = x_ref[...]
    ms = jnp.mean(x * x, axis=-1, keepdims=True)
    o_ref[...] = (x * lax.rsqrt(ms + NORM_EPS) * g_ref[...]).astype(o_ref.dtype)


def _rmsnorm(x, g, tm=512):
    m, d = x.shape
    return pl.pallas_call(
        _rmsnorm_kernel,
        grid=(m // tm,),
        in_specs=[pl.BlockSpec((tm, d), lambda i: (i, 0)), pl.BlockSpec((1, d), lambda i: (0, 0))],
        out_specs=pl.BlockSpec((tm, d), lambda i: (i, 0)),
        out_shape=jax.ShapeDtypeStruct((m, d), BF16),
        compiler_params=_params("parallel"),
        name="rmsnorm",
    )(x, g.reshape(1, d))


def _mm_kernel(a_ref, w_ref, *refs, nk, act, w_rows_are_outputs, ncast):
    o_ref = refs[ncast]
    for src_ref, dst_ref in zip(refs[:ncast], refs[ncast + 1:]):
        dst_ref[...] = src_ref[...].astype(dst_ref.dtype)

    def part():
        w = w_ref[...].astype(BF16)
        if w_rows_are_outputs:
            return lax.dot_general(a_ref[...], w, NT_DIMS, preferred_element_type=F32)
        return jnp.dot(a_ref[...], w, preferred_element_type=F32)

    if nk == 1:
        acc = part()
        if act == "relu2":
            r = jnp.maximum(acc, 0.0)
            acc = r * r
        o_ref[...] = acc.astype(o_ref.dtype)
        return
    k = pl.program_id(2)

    @pl.when(k == 0)
    def _():
        o_ref[...] = part()

    @pl.when(k > 0)
    def _():
        o_ref[...] += part()


def _a_buffering(nk):
    return pl.Buffered(1) if nk == 1 else None


def _slab_specs(arrays, nsteps, step_of):
    specs, shapes = [], []
    for arr in arrays:
        rows, cols = arr.shape
        assert rows % nsteps == 0 and (rows // nsteps) % 16 == 0
        specs.append(pl.BlockSpec((rows // nsteps, cols), lambda *idx: (step_of(*idx), 0)))
        shapes.append(jax.ShapeDtypeStruct((rows, cols), BF16))
    return specs, shapes


def _matmul(a, w, *, tm, tn, tk, out_dtype, n_out=None, act=None, w_rows_are_outputs=False, casts=(),
            name="matmul"):
    m, kdim = a.shape
    if n_out is None:
        n_out = w.shape[0] if w_rows_are_outputs else w.shape[1]
    n = n_out
    nk = kdim // tk
    assert nk == 1 or (act is None and out_dtype == F32)
    if w_rows_are_outputs:
        w_spec = pl.BlockSpec((tn, tk), lambda i, j, k: (j, k))
    else:
        w_spec = pl.BlockSpec((tk, tn), lambda i, j, k: (k, j))
    grid = (m // tm, n // tn, nk)
    cast_specs, cast_shapes = _slab_specs(casts, grid[0] * grid[1] * grid[2],
                                          lambda i, j, k: (i * grid[1] + j) * grid[2] + k)
    out = pl.pallas_call(
        functools.partial(_mm_kernel, nk=nk, act=act, w_rows_are_outputs=w_rows_are_outputs, ncast=len(casts)),
        grid=grid,
        in_specs=[pl.BlockSpec((tm, tk), lambda i, j, k: (i, k), pipeline_mode=_a_buffering(nk)), w_spec,
                  *cast_specs],
        out_specs=[pl.BlockSpec((tm, tn), lambda i, j, k: (i, j)), *cast_specs],
        out_shape=[jax.ShapeDtypeStruct((m, n), out_dtype), *cast_shapes],
        compiler_params=_params("parallel", "parallel", "arbitrary"),
        name=name,
    )(a, w, *casts)
    return out if casts else out[0]


def _mm_pair_kernel(a1_ref, a2_ref, w1_ref, w2_ref, o_ref):
    acc = jnp.dot(a1_ref[...], w1_ref[...].astype(BF16), preferred_element_type=F32)
    acc = acc + jnp.dot(a2_ref[...], w2_ref[...].astype(BF16), preferred_element_type=F32)
    o_ref[...] = acc.astype(o_ref.dtype)


def _matmul_pair(a1, a2, w, *, tm, tn, out_dtype, name):
    m, k1 = a1.shape
    _, k2 = a2.shape
    assert k1 == k2 and w.shape[0] == k1 + k2
    n = w.shape[1]
    return pl.pallas_call(
        _mm_pair_kernel,
        grid=(m // tm, n // tn),
        in_specs=[pl.BlockSpec((tm, k1), lambda i, j: (i, 0), pipeline_mode=pl.Buffered(1)),
                  pl.BlockSpec((tm, k2), lambda i, j: (i, 0), pipeline_mode=pl.Buffered(1)),
                  pl.BlockSpec((k1, tn), lambda i, j: (0, j)), pl.BlockSpec((k2, tn), lambda i, j: (1, j))],
        out_specs=pl.BlockSpec((tm, tn), lambda i, j: (i, j)),
        out_shape=jax.ShapeDtypeStruct((m, n), out_dtype),
        compiler_params=_params("parallel", "parallel"),
        name=name,
    )(a1, a2, w, w)


def _gate_kernel(a_ref, w_ref, alog_ref, dtb_ref, g_ref, beta_ref):
    p = lax.dot_general(a_ref[...], w_ref[...].astype(BF16), NT_DIMS, preferred_element_type=F32)
    a = p + dtb_ref[...]
    softplus = jnp.maximum(a, 0.0) + jnp.log(1.0 + jnp.exp(-jnp.abs(a)))
    g_ref[...] = -jnp.exp(alog_ref[...]) * softplus
    beta_ref[...] = _sigmoid(pltpu.roll(p, 128 - GDN_HEADS, axis=1))


def _gate_proj(xn, w_t, a_log, dt_bias, *, row0, tm=1024):
    m, d = xn.shape
    row = lambda i: (i, 0)
    fixed = lambda i: (0, 0)
    return pl.pallas_call(
        _gate_kernel,
        grid=(m // tm,),
        in_specs=[pl.BlockSpec((tm, d), row), pl.BlockSpec((128, d), lambda i: (row0 // 128, 0)),
                  pl.BlockSpec((1, 128), fixed), pl.BlockSpec((1, 128), fixed)],
        out_specs=[pl.BlockSpec((tm, 128), row), pl.BlockSpec((tm, 128), row)],
        out_shape=[jax.ShapeDtypeStruct((m, 128), F32), jax.ShapeDtypeStruct((m, 128), F32)],
        compiler_params=_params("parallel"),
        name="gate_proj",
    )(xn, w_t, a_log, dt_bias)


def _gdn_kernel(q_ref, k_ref, v_ref, z_ref, qh_ref, kh_ref, vh_ref, g_ref, b_ref, wq_ref, wk_ref, wv_ref, nw_ref,
                side_ref, o_ref, side_out_ref, s_ref, u_ref, wqd_ref, kdt_ref, qkm_ref, egl_ref, *, hb, nchunk):
    side_out_ref[...] = side_ref[...].astype(side_out_ref.dtype)
    head0 = pl.program_id(1) * hb
    tile = pl.program_id(2)

    @pl.when(tile == 0)
    def _():
        s_ref[...] = jnp.zeros_like(s_ref)

    row = lax.broadcasted_iota(jnp.int32, (CHUNK, CHUNK), 0)
    col = lax.broadcasted_iota(jnp.int32, (CHUNK, CHUNK), 1)
    tri_incl = row >= col
    tri_strict = row > col
    tri_f = tri_incl.astype(F32)
    eye = (row == col).astype(F32)
    shift = (128 - head0) % 128
    nt = (((1,), (1,)), ((), ()))
    keep_halo = jnp.where(tile > 0, 1.0, 0.0)

    heads = range(hb)
    head_lanes = [slice(j * HEAD_DIM, (j + 1) * HEAD_DIM) for j in heads]

    def precompute(c, first_in_tile=False):
        r0 = pl.multiple_of(c * CHUNK, CHUNK)
        rows = pl.ds(r0, CHUNK)

        graw = pltpu.roll(g_ref[0, rows, :], shift, axis=1)
        beta = pltpu.roll(b_ref[0, rows, :], shift, axis=1)
        gc = jnp.dot(tri_f, graw, preferred_element_type=F32, precision=lax.Precision.HIGHEST)
        eg = jnp.exp(gc)
        g_last = gc[CHUNK - 1:CHUNK, :]
        eg_rest = jnp.exp(g_last - gc)
        gc_t = gc.T

        def conv_silu(ref, halo_ref, w_ref, lanes):
            cur = ref[0, rows, lanes]
            w = w_ref[:, lanes]
            acc = cur * w[GDN_CONV - 1:GDN_CONV, :]
            if first_in_tile:
                prev = halo_ref[0, :, lanes] * keep_halo
            else:
                prev = ref[0, pl.ds(pl.multiple_of(r0 - 8, 8), 8), lanes]
            ext = jnp.concatenate([prev, cur], axis=0)
            for t in range(GDN_CONV - 1):
                back = GDN_CONV - 1 - t
                acc = acc + ext[8 - back:8 - back + CHUNK, :] * w[t:t + 1, :]
            return _silu(acc)

        q = [conv_silu(q_ref, qh_ref, wq_ref, head_lanes[j]) for j in heads]
        k = [conv_silu(k_ref, kh_ref, wk_ref, head_lanes[j]) for j in heads]
        v = [conv_silu(v_ref, vh_ref, wv_ref, head_lanes[j]) for j in heads]
        q = [x * (lax.rsqrt(jnp.sum(x * x, axis=-1, keepdims=True) + L2_EPS) * (HEAD_DIM ** -0.5)) for x in q]
        k = [x * lax.rsqrt(jnp.sum(x * x, axis=-1, keepdims=True) + L2_EPS) for x in k]
        kb = [x.astype(BF16) for x in k]
        kk = [lax.dot_general(kb[j], kb[j], nt, preferred_element_type=F32) for j in heads]
        qk = [lax.dot_general(q[j].astype(BF16), kb[j], nt, preferred_element_type=F32) for j in heads]
        yield

        beta_col = [beta[:, j:j + 1] for j in heads]
        decay = [jnp.where(tri_incl, jnp.exp(gc[:, j:j + 1] - gc_t[j:j + 1, :]), 0.0) for j in heads]
        low = [jnp.where(tri_strict, beta_col[j] * kk[j] * decay[j], 0.0) for j in heads]

        inv = [eye - x for x in low]
        pw = low
        for _ in range(5):
            pwb = [x.astype(BF16) for x in pw]
            pw = [jnp.dot(x, x, preferred_element_type=F32) for x in pwb]
            inv = [inv[j] + jnp.dot(inv[j].astype(BF16), pw[j].astype(BF16), preferred_element_type=F32)
                   for j in heads]
            yield

        rhs = [jnp.concatenate([v[j] * beta_col[j], k[j] * (beta_col[j] * eg[:, j:j + 1])], axis=1) for j in heads]
        sol = [jnp.dot(inv[j].astype(BF16), rhs[j].astype(BF16), preferred_element_type=F32) for j in heads]
        yield
        egl_ref[c] = jnp.broadcast_to(jnp.exp(g_last), (8, 128))
        for j in heads:
            u_ref[j, rows, :] = sol[j][:, :HEAD_DIM]
            wqd_ref[j, c, :CHUNK, :] = sol[j][:, HEAD_DIM:].astype(BF16)
            wqd_ref[j, c, CHUNK:, :] = (q[j] * eg[:, j:j + 1]).astype(BF16)
            kdt_ref[j, c] = (k[j] * eg_rest[:, j:j + 1]).T.astype(BF16)
            qkm_ref[j, rows, :] = (qk[j] * decay[j]).astype(BF16)

    def scan(c):
        r0 = pl.multiple_of(c * CHUNK, CHUNK)
        rows = pl.ds(r0, CHUNK)
        egl = egl_ref[c]
        state = [s_ref[j] for j in heads]
        u = [u_ref[j, rows, :] for j in heads]
        kdt = [kdt_ref[j, c] for j in heads]
        qkm = [qkm_ref[j, rows, :] for j in heads]
        proj = [jnp.dot(wqd_ref[j, c], state[j].astype(BF16), preferred_element_type=F32) for j in heads]
        yield
        v_new = [(u[j] - proj[j][:CHUNK]).astype(BF16) for j in heads]
        for j in heads:
            s_ref[j] = state[j] * egl[0:1, j:j + 1] + jnp.dot(kdt[j], v_new[j], preferred_element_type=F32)
        o = [proj[j][CHUNK:] + jnp.dot(qkm[j], v_new[j], preferred_element_type=F32) for j in heads]
        yield
        for j in heads:
            ms = jnp.mean(o[j] * o[j], axis=-1, keepdims=True)
            z = z_ref[0, rows, head_lanes[j]]
            out = o[j] * lax.rsqrt(ms + NORM_EPS) * nw_ref[...] * _silu(z)
            o_ref[0, rows, head_lanes[j]] = out.astype(o_ref.dtype)

    def interleave(*parts):
        parts = list(parts)
        while parts:
            parts = [p for p in parts if next(p, StopIteration) is not StopIteration]

    def chain(*parts):
        for p in parts:
            yield from p

    assert nchunk % 2 == 0
    interleave(precompute(0, first_in_tile=True), precompute(1))

    def steady(i, carry):
        c = 2 * i
        interleave(precompute(c + 2), precompute(c + 3), chain(scan(c), scan(c + 1)))
        return carry

    lax.fori_loop(0, nchunk // 2 - 1, steady, 0)
    interleave(chain(scan(nchunk - 2), scan(nchunk - 1)))


def _gdn(proj, g, beta, conv_w, norm_w, side, *, bsz, seq, hb=8, ts=512):
    nblk = GDN_WIDTH // (HEAD_DIM * hb)
    wblk = HEAD_DIM * hb
    nchunk = ts // CHUNK
    ntile = seq // ts
    (side_spec,), (side_shape,) = _slab_specs([side], bsz * nblk * ntile, lambda b, h, t: (b * nblk + h) * ntile + t)

    def col(group):
        return pl.BlockSpec((1, ts, wblk), lambda b, h, t: (b, t, group * nblk + h))

    def halo(group):
        return pl.BlockSpec((1, 8, wblk), lambda b, h, t: (b, jnp.maximum(t * (ts // 8) - 1, 0), group * nblk + h))

    def wcol(group):
        return pl.BlockSpec((GDN_CONV, wblk), lambda b, h, t: (0, group * nblk + h))

    gspec = pl.BlockSpec((1, ts, 128), lambda b, h, t: (b, t, 0))
    return pl.pallas_call(
        functools.partial(_gdn_kernel, hb=hb, nchunk=nchunk),
        grid=(bsz, nblk, ntile),
        in_specs=[col(0), col(1), col(2), col(3), halo(0), halo(1), halo(2), gspec, gspec,
                  wcol(0), wcol(1), wcol(2), pl.BlockSpec((1, HEAD_DIM), lambda b, h, t: (0, 0)), side_spec],
        out_specs=[pl.BlockSpec((1, ts, wblk), lambda b, h, t: (b, t, h)), side_spec],
        out_shape=[jax.ShapeDtypeStruct((bsz, seq, GDN_WIDTH), BF16), side_shape],
        scratch_shapes=[pltpu.VMEM((hb, HEAD_DIM, HEAD_DIM), F32),
                        pltpu.VMEM((hb, ts, HEAD_DIM), F32),
                        pltpu.VMEM((hb, nchunk, 2 * CHUNK, HEAD_DIM), BF16),
                        pltpu.VMEM((hb, nchunk, HEAD_DIM, CHUNK), BF16),
                        pltpu.VMEM((hb, ts, CHUNK), BF16),
                        pltpu.VMEM((nchunk, 8, 128), F32)],
        compiler_params=_params("parallel", "parallel", "arbitrary"),
        name="gdn",
    )(proj, proj, proj, proj, proj, proj, proj, g, beta, conv_w, conv_w, conv_w, norm_w, side)


def _sc_kernel(a_ref, wb0_ref, wb1_ref, wc0_ref, wc1_ref, wh0_ref, wh1_ref, cw_ref, side_ref, o_ref, side_out_ref,
               p_ref, b_ref, *, row_off, rows_per_step, nstep):
    side_out_ref[...] = side_ref[...].astype(side_out_ref.dtype)

    def proj(w0_ref, w1_ref):
        w_t = jnp.concatenate([w0_ref[row_off:, :], w1_ref[...]], axis=0)
        return lax.dot_general(a_ref[...], w_t, NT_DIMS, preferred_element_type=F32)

    p_ref[0:8, :] = jnp.zeros((8, p_ref.shape[1]), F32)
    p_ref[8:, :] = proj(wc0_ref, wc1_ref)
    p_ref[8:, :] *= proj(wh0_ref, wh1_ref)
    b_ref[...] = proj(wb0_ref, wb1_ref)
    w = cw_ref[...]

    def step(i, carry):
        r0 = pl.multiple_of(i * rows_per_step, rows_per_step)
        ext = p_ref[pl.ds(r0, rows_per_step + 8), :]
        acc = ext[8:, :] * w[SC_CONV - 1:SC_CONV, :]
        for t in range(SC_CONV - 1):
            off = 8 - (SC_CONV - 1) + t
            acc = acc + ext[off:off + rows_per_step, :] * w[t:t + 1, :]
        rows = pl.ds(r0, rows_per_step)
        o_ref[rows, :] = (b_ref[rows, :] * acc).astype(o_ref.dtype)
        return carry

    lax.fori_loop(0, nstep, step, 0)


def _sc_proj_conv(xn, w_ext_t, conv_w, side, *, seq, row_off, tc=256, rows_per_step=128):
    m, d = xn.shape
    nblk = SC_WIDTH // tc
    sub = tc // row_off
    (side_spec,), (side_shape,) = _slab_specs([side], (m // seq) * nblk, lambda b, j: b * nblk + j)

    def wrows(group):
        lead = pl.BlockSpec((tc, d), lambda b, j: (group * nblk + j, 0))
        tail = pl.BlockSpec((row_off, d), lambda b, j: ((group * nblk + j + 1) * sub, 0))
        return [lead, tail]

    return pl.pallas_call(
        functools.partial(_sc_kernel, row_off=row_off, rows_per_step=rows_per_step, nstep=seq // rows_per_step),
        grid=(m // seq, nblk),
        in_specs=[pl.BlockSpec((seq, d), lambda b, j: (b, 0), pipeline_mode=pl.Buffered(1)),
                  *wrows(0), *wrows(1), *wrows(2), pl.BlockSpec((SC_CONV, tc), lambda b, j: (0, j)), side_spec],
        out_specs=[pl.BlockSpec((seq, tc), lambda b, j: (b, j)), side_spec],
        out_shape=[jax.ShapeDtypeStruct((m, SC_WIDTH), BF16), side_shape],
        scratch_shapes=[pltpu.VMEM((8 + seq, tc), F32), pltpu.VMEM((seq, tc), F32)],
        compiler_params=_params("parallel", "parallel"),
        name="sc_proj_conv",
    )(xn, *([w_ext_t] * 6), conv_w, side)


def _res_norm_kernel(t_ref, r_ref, gpost_ref, gnext_ref, h_ref, hn_ref):
    t = t_ref[...]
    h = r_ref[...] + t * lax.rsqrt(jnp.mean(t * t, axis=-1, keepdims=True) + NORM_EPS) * gpost_ref[...]
    h_ref[...] = h
    hn = h * lax.rsqrt(jnp.mean(h * h, axis=-1, keepdims=True) + NORM_EPS) * gnext_ref[...]
    hn_ref[...] = hn.astype(hn_ref.dtype)


def _res_norm(t, resid, g_post, g_next, tm=256):
    m, d = t.shape
    row = lambda i: (i, 0)
    fixed = lambda i: (0, 0)
    return pl.pallas_call(
        _res_norm_kernel,
        grid=(m // tm,),
        in_specs=[pl.BlockSpec((tm, d), row), pl.BlockSpec((tm, d), row),
                  pl.BlockSpec((1, d), fixed), pl.BlockSpec((1, d), fixed)],
        out_specs=[pl.BlockSpec((tm, d), row), pl.BlockSpec((tm, d), row)],
        out_shape=[jax.ShapeDtypeStruct((m, d), F32), jax.ShapeDtypeStruct((m, d), BF16)],
        compiler_params=_params("parallel"),
        name="res_norm",
    )(t, resid, g_post.reshape(1, d), g_next.reshape(1, d))


def _res_final_kernel(t_ref, r_ref, g_ref, o_ref):
    t = t_ref[...]
    o_ref[...] = r_ref[...] + t * lax.rsqrt(jnp.mean(t * t, axis=-1, keepdims=True) + NORM_EPS) * g_ref[...]


def _res_final(t, resid, g, tm=512):
    m, d = t.shape
    row = lambda i: (i, 0)
    return pl.pallas_call(
        _res_final_kernel,
        grid=(m // tm,),
        in_specs=[pl.BlockSpec((tm, d), row), pl.BlockSpec((tm, d), row), pl.BlockSpec((1, d), lambda i: (0, 0))],
        out_specs=pl.BlockSpec((tm, d), row),
        out_shape=jax.ShapeDtypeStruct((m, d), F32),
        compiler_params=_params("parallel"),
        name="res_final",
    )(t, resid, g.reshape(1, d))


def _layer(h, norm_mix_pre, w_in, conv_qkv_w, a_log, dt_bias, gdn_norm_w, conv_sc_w, w_out,
           norm_mix_post, norm_mlp_pre, w_up, w_down, norm_mlp_post, *, bsz, seq):
    m = bsz * seq
    n_gdn = 4 * GDN_WIDTH
    n_ab = 2 * GDN_HEADS

    xn = _rmsnorm(h, norm_mix_pre)

    w_in_t = w_in.T
    w_ext_t = w_in_t[n_gdn:].astype(BF16)
    lane_pad = jnp.zeros((1, 128 - GDN_HEADS), F32)
    a_log_p = jnp.concatenate([a_log.reshape(1, -1), lane_pad], axis=1)
    dt_bias_p = jnp.concatenate([dt_bias.reshape(1, -1), lane_pad], axis=1)

    proj, w_up_b = _matmul(xn, w_in_t, tm=2048, tn=512, tk=D_MODEL, out_dtype=F32, n_out=n_gdn,
                           w_rows_are_outputs=True, casts=(w_up,), name="in_proj")
    g, beta = _gate_proj(xn, w_in_t, a_log_p, dt_bias_p, row0=n_gdn)

    proj3 = proj.reshape(bsz, seq, -1)
    gdn_out, w_down_b = _gdn(proj3, g.reshape(bsz, seq, 128), beta.reshape(bsz, seq, 128), conv_qkv_w,
                             gdn_norm_w.reshape(1, HEAD_DIM), w_down, bsz=bsz, seq=seq)
    sc_out, w_out_b = _sc_proj_conv(xn, w_ext_t, conv_sc_w, w_out, seq=seq, row_off=n_ab)

    mix = _matmul_pair(gdn_out.reshape(m, -1), sc_out, w_out_b, tm=2048, tn=512, out_dtype=F32, name="out_proj")
    h, hn = _res_norm(mix, h, norm_mix_post, norm_mlp_pre)

    hid = _matmul(hn, w_up_b, tm=2048, tn=1024, tk=D_MODEL, out_dtype=BF16, act="relu2", name="mlp_up")
    ff = _matmul(hid, w_down_b, tm=2048, tn=1024, tk=2048, out_dtype=F32, name="mlp_down")
    return _res_final(ff, h, norm_mlp_post)


def kernel(x, norm_mix_pre, w_in, conv_qkv_w, a_log, dt_bias, gdn_norm_w, conv_sc_w, w_out, norm_mix_post,
           norm_mlp_pre, w_up, w_down, norm_mlp_post):
    bsz, seq, d = x.shape
    h = x.reshape(bsz * seq, d)
    for l in range(norm_mix_pre.shape[0]):
        h = _layer(h, norm_mix_pre[l], w_in[l], conv_qkv_w[l], a_log[l], dt_bias[l], gdn_norm_w[l], conv_sc_w[l],
                   w_out[l], norm_mix_post[l], norm_mlp_pre[l], w_up[l], w_down[l], norm_mlp_post[l],
                   bsz=bsz, seq=seq)
    return h.reshape(bsz, seq, d)
```

```python
import functools

import jax
import jax.numpy as jnp
from jax import lax
from jax.experimental import pallas as pl
from jax.experimental.pallas import tpu as pltpu

F32 = jnp.float32
BF16 = jnp.bfloat16

D_MODEL = 4096
CHUNK = 128
HEAD_DIM = 128
GDN_WIDTH = 2048
GDN_HEADS = GDN_WIDTH // HEAD_DIM
SC_WIDTH = 2048
GDN_CONV = 4
SC_CONV = 3
NORM_EPS = 1e-6
L2_EPS = 1e-6

V7X_VMEM_LIMIT_BYTES = 60 * 1024 * 1024

NT_DIMS = (((1,), (1,)), ((), ()))


def _params(*sem):
    return pltpu.CompilerParams(dimension_semantics=sem, vmem_limit_bytes=V7X_VMEM_LIMIT_BYTES)


def _sigmoid(x):
    return 1.0 / (1.0 + jnp.exp(-x))


def _rmsnorm_kernel(x_ref, g_ref, o_ref):
    x = x_ref[...]
    ms = jnp.mean(x * x, axis=-1, keepdims=True)
    o_ref[...] = (x * lax.rsqrt(ms + NORM_EPS) * g_ref[...]).astype(o_ref.dtype)


def _rmsnorm(x, g, tm=256):
    m, d = x.shape
    return pl.pallas_call(
        _rmsnorm_kernel,
        grid=(m // tm,),
        in_specs=[pl.BlockSpec((tm, d), lambda i: (i, 0)), pl.BlockSpec((1, d), lambda i: (0, 0))],
        out_specs=pl.BlockSpec((tm, d), lambda i: (i, 0)),
        out_shape=jax.ShapeDtypeStruct((m, d), BF16),
        compiler_params=_params("parallel"),
        name="rmsnorm",
    )(x, g.reshape(1, d))


def _mm_kernel(a_ref, w_ref, o_ref, *, nk, act, w_rows_are_outputs):
    def part():
        w = w_ref[...].astype(BF16)
        if w_rows_are_outputs:
            return lax.dot_general(a_ref[...], w, NT_DIMS, preferred_element_type=F32)
        return jnp.dot(a_ref[...], w, preferred_element_type=F32)

    if nk == 1:
        acc = part()
        if act == "relu2":
            r = jnp.maximum(acc, 0.0)
            acc = r * r
        o_ref[...] = acc.astype(o_ref.dtype)
        return
    k = pl.program_id(2)

    @pl.when(k == 0)
    def _():
        o_ref[...] = part()

    @pl.when(k > 0)
    def _():
        o_ref[...] += part()


def _a_buffering(nk):
    return pl.Buffered(1) if nk == 1 else None


def _matmul(a, w, *, tm, tn, tk, out_dtype, n_out=None, act=None, w_rows_are_outputs=False, name="matmul"):
    m, kdim = a.shape
    if n_out is None:
        n_out = w.shape[0] if w_rows_are_outputs else w.shape[1]
    n = n_out
    nk = kdim // tk
    assert nk == 1 or (act is None and out_dtype == F32)
    if w_rows_are_outputs:
        w_spec = pl.BlockSpec((tn, tk), lambda i, j, k: (j, k))
    else:
        w_spec = pl.BlockSpec((tk, tn), lambda i, j, k: (k, j))
    return pl.pallas_call(
        functools.partial(_mm_kernel, nk=nk, act=act, w_rows_are_outputs=w_rows_are_outputs),
        grid=(m // tm, n // tn, nk),
        in_specs=[pl.BlockSpec((tm, tk), lambda i, j, k: (i, k), pipeline_mode=_a_buffering(nk)), w_spec],
        out_specs=pl.BlockSpec((tm, tn), lambda i, j, k: (i, j)),
        out_shape=jax.ShapeDtypeStruct((m, n), out_dtype),
        compiler_params=_params("parallel", "parallel", "arbitrary"),
        name=name,
    )(a, w)


def _mm_pair_kernel(a1_ref, a2_ref, w1_ref, w2_ref, o_ref):
    acc = jnp.dot(a1_ref[...], w1_ref[...].astype(BF16), preferred_element_type=F32)
    acc = acc + jnp.dot(a2_ref[...], w2_ref[...].astype(BF16), preferred_element_type=F32)
    o_ref[...] = acc.astype(o_ref.dtype)


def _matmul_pair(a1, a2, w, *, tm, tn, out_dtype, name):
    m, k1 = a1.shape
    _, k2 = a2.shape
    assert k1 == k2 and w.shape[0] == k1 + k2
    n = w.shape[1]
    return pl.pallas_call(
        _mm_pair_kernel,
        grid=(m // tm, n // tn),
        in_specs=[pl.BlockSpec((tm, k1), lambda i, j: (i, 0), pipeline_mode=pl.Buffered(1)),
                  pl.BlockSpec((tm, k2), lambda i, j: (i, 0), pipeline_mode=pl.Buffered(1)),
                  pl.BlockSpec((k1, tn), lambda i, j: (0, j)), pl.BlockSpec((k2, tn), lambda i, j: (1, j))],
        out_specs=pl.BlockSpec((tm, tn), lambda i, j: (i, j)),
        out_shape=jax.ShapeDtypeStruct((m, n), out_dtype),
        compiler_params=_params("parallel", "parallel"),
        name=name,
    )(a1, a2, w, w)


def _gate_kernel(a_ref, w_ref, alog_ref, dtb_ref, g_ref, beta_ref):
    p = lax.dot_general(a_ref[...], w_ref[...].astype(BF16), NT_DIMS, preferred_element_type=F32)
    a = p + dtb_ref[...]
    softplus = jnp.maximum(a, 0.0) + jnp.log(1.0 + jnp.exp(-jnp.abs(a)))
    g_ref[...] = -jnp.exp(alog_ref[...]) * softplus
    beta_ref[...] = _sigmoid(pltpu.roll(p, 128 - GDN_HEADS, axis=1))


def _gate_proj(xn, w_t, a_log, dt_bias, *, row0, tm=1024):
    m, d = xn.shape
    row = lambda i: (i, 0)
    fixed = lambda i: (0, 0)
    return pl.pallas_call(
        _gate_kernel,
        grid=(m // tm,),
        in_specs=[pl.BlockSpec((tm, d), row), pl.BlockSpec((128, d), lambda i: (row0 // 128, 0)),
                  pl.BlockSpec((1, 128), fixed), pl.BlockSpec((1, 128), fixed)],
        out_specs=[pl.BlockSpec((tm, 128), row), pl.BlockSpec((tm, 128), row)],
        out_shape=[jax.ShapeDtypeStruct((m, 128), F32), jax.ShapeDtypeStruct((m, 128), F32)],
        compiler_params=_params("parallel"),
        name="gate_proj",
    )(xn, w_t, a_log, dt_bias)


def _gdn_kernel(q_ref, k_ref, v_ref, z_ref, qh_ref, kh_ref, vh_ref, g_ref, b_ref, wq_ref, wk_ref, wv_ref, nw_ref,
                o_ref, s_ref, u_ref, wqd_ref, kdt_ref, qkm_ref, egl_ref, *, hb, nchunk):
    head0 = pl.program_id(1) * hb
    tile = pl.program_id(2)

    @pl.when(tile == 0)
    def _():
        s_ref[...] = jnp.zeros_like(s_ref)

    row = lax.broadcasted_iota(jnp.int32, (CHUNK, CHUNK), 0)
    col = lax.broadcasted_iota(jnp.int32, (CHUNK, CHUNK), 1)
    tri_incl = row >= col
    tri_strict = row > col
    tri_f = tri_incl.astype(F32)
    eye = (row == col).astype(F32)
    shift = (128 - head0) % 128
    nt = (((1,), (1,)), ((), ()))
    keep_halo = jnp.where(tile > 0, 1.0, 0.0)

    heads = range(hb)
    head_lanes = [slice(j * HEAD_DIM, (j + 1) * HEAD_DIM) for j in heads]

    def precompute(c, first_in_tile=False):
        r0 = pl.multiple_of(c * CHUNK, CHUNK)
        rows = pl.ds(r0, CHUNK)

        graw = pltpu.roll(g_ref[0, rows, :], shift, axis=1)
        beta = pltpu.roll(b_ref[0, rows, :], shift, axis=1)
        gc = jnp.dot(tri_f, graw, preferred_element_type=F32, precision=lax.Precision.HIGHEST)
        eg = jnp.exp(gc)
        g_last = gc[CHUNK - 1:CHUNK, :]
        eg_rest = jnp.exp(g_last - gc)
        gc_t = gc.T

        def conv_silu(ref, halo_ref, w_ref, lanes):
            cur = ref[0, rows, lanes]
            w = w_ref[:, lanes]
            acc = cur * w[GDN_CONV - 1:GDN_CONV, :]
            if first_in_tile:
                prev = halo_ref[0, :, lanes] * keep_halo
            else:
                prev = ref[0, pl.ds(pl.multiple_of(r0 - 8, 8), 8), lanes]
            ext = jnp.concatenate([prev, cur], axis=0)
            for t in range(GDN_CONV - 1):
                back = GDN_CONV - 1 - t
                acc = acc + ext[8 - back:8 - back + CHUNK, :] * w[t:t + 1, :]
            return acc * _sigmoid(acc)

        q = [conv_silu(q_ref, qh_ref, wq_ref, head_lanes[j]) for j in heads]
        k = [conv_silu(k_ref, kh_ref, wk_ref, head_lanes[j]) for j in heads]
        v = [conv_silu(v_ref, vh_ref, wv_ref, head_lanes[j]) for j in heads]
        q = [x * (lax.rsqrt(jnp.sum(x * x, axis=-1, keepdims=True) + L2_EPS) * (HEAD_DIM ** -0.5)) for x in q]
        k = [x * lax.rsqrt(jnp.sum(x * x, axis=-1, keepdims=True) + L2_EPS) for x in k]
        kb = [x.astype(BF16) for x in k]
        kk = [lax.dot_general(kb[j], kb[j], nt, preferred_element_type=F32) for j in heads]
        qk = [lax.dot_general(q[j].astype(BF16), kb[j], nt, preferred_element_type=F32) for j in heads]
        yield

        beta_col = [beta[:, j:j + 1] for j in heads]
        decay = [jnp.where(tri_incl, jnp.exp(gc[:, j:j + 1] - gc_t[j:j + 1, :]), 0.0) for j in heads]
        low = [jnp.where(tri_strict, beta_col[j] * kk[j] * decay[j], 0.0) for j in heads]

        inv = [eye - x for x in low]
        pw = low
        for _ in range(CHUNK.bit_length() - 2):
            pwb = [x.astype(BF16) for x in pw]
            pw = [jnp.dot(x, x, preferred_element_type=F32) for x in pwb]
            inv = [inv[j] + jnp.dot(inv[j].astype(BF16), pw[j].astype(BF16), preferred_element_type=F32)
                   for j in heads]
            yield

        rhs = [jnp.concatenate([v[j] * beta_col[j], k[j] * (beta_col[j] * eg[:, j:j + 1])], axis=1) for j in heads]
        sol = [jnp.dot(inv[j].astype(BF16), rhs[j].astype(BF16), preferred_element_type=F32) for j in heads]
        yield
        egl_ref[c] = jnp.broadcast_to(jnp.exp(g_last), (8, 128))
        for j in heads:
            u_ref[j, rows, :] = sol[j][:, :HEAD_DIM]
            wqd_ref[j, c, :CHUNK, :] = sol[j][:, HEAD_DIM:].astype(BF16)
            wqd_ref[j, c, CHUNK:, :] = (q[j] * eg[:, j:j + 1]).astype(BF16)
            kdt_ref[j, c] = (k[j] * eg_rest[:, j:j + 1]).T.astype(BF16)
            qkm_ref[j, rows, :] = (qk[j] * decay[j]).astype(BF16)

    def scan(c):
        r0 = pl.multiple_of(c * CHUNK, CHUNK)
        rows = pl.ds(r0, CHUNK)
        egl = egl_ref[c]
        state = [s_ref[j] for j in heads]
        u = [u_ref[j, rows, :] for j in heads]
        kdt = [kdt_ref[j, c] for j in heads]
        qkm = [qkm_ref[j, rows, :] for j in heads]
        proj = [jnp.dot(wqd_ref[j, c], state[j].astype(BF16), preferred_element_type=F32) for j in heads]
        yield
        v_new = [(u[j] - proj[j][:CHUNK]).astype(BF16) for j in heads]
        for j in heads:
            s_ref[j] = state[j] * egl[0:1, j:j + 1] + jnp.dot(kdt[j], v_new[j], preferred_element_type=F32)
        o = [proj[j][CHUNK:] + jnp.dot(qkm[j], v_new[j], preferred_element_type=F32) for j in heads]
        yield
        for j in heads:
            ms = jnp.mean(o[j] * o[j], axis=-1, keepdims=True)
            z = z_ref[0, rows, head_lanes[j]]
            out = o[j] * lax.rsqrt(ms + NORM_EPS) * nw_ref[...] * (z * _sigmoid(z))
            o_ref[0, rows, head_lanes[j]] = out.astype(o_ref.dtype)

    def interleave(*parts):
        parts = list(parts)
        while parts:
            parts = [p for p in parts if next(p, StopIteration) is not StopIteration]

    def chain(*parts):
        for p in parts:
            yield from p

    assert nchunk % 2 == 0
    interleave(precompute(0, first_in_tile=True), precompute(1))

    def steady(i, carry):
        c = 2 * i
        interleave(precompute(c + 2), precompute(c + 3), chain(scan(c), scan(c + 1)))
        return carry

    lax.fori_loop(0, nchunk // 2 - 1, steady, 0)
    interleave(chain(scan(nchunk - 2), scan(nchunk - 1)))


def _gdn(proj, g, beta, conv_w, norm_w, *, bsz, seq, hb=8, ts=512):
    nblk = GDN_WIDTH // (HEAD_DIM * hb)
    wblk = HEAD_DIM * hb
    nchunk = ts // CHUNK

    def col(group):
        return pl.BlockSpec((1, ts, wblk), lambda b, h, t: (b, t, group * nblk + h))

    def halo(group):
        return pl.BlockSpec((1, 8, wblk), lambda b, h, t: (b, jnp.maximum(t * (ts // 8) - 1, 0), group * nblk + h))

    def wcol(group):
        return pl.BlockSpec((GDN_CONV, wblk), lambda b, h, t: (0, group * nblk + h))

    gspec = pl.BlockSpec((1, ts, 128), lambda b, h, t: (b, t, 0))
    return pl.pallas_call(
        functools.partial(_gdn_kernel, hb=hb, nchunk=nchunk),
        grid=(bsz, nblk, seq // ts),
        in_specs=[col(0), col(1), col(2), col(3), halo(0), halo(1), halo(2), gspec, gspec,
                  wcol(0), wcol(1), wcol(2), pl.BlockSpec((1, HEAD_DIM), lambda b, h, t: (0, 0))],
        out_specs=pl.BlockSpec((1, ts, wblk), lambda b, h, t: (b, t, h)),
        out_shape=jax.ShapeDtypeStruct((bsz, seq, GDN_WIDTH), BF16),
        scratch_shapes=[pltpu.VMEM((hb, HEAD_DIM, HEAD_DIM), F32),
                        pltpu.VMEM((hb, ts, HEAD_DIM), F32),
                        pltpu.VMEM((hb, nchunk, 2 * CHUNK, HEAD_DIM), BF16),
                        pltpu.VMEM((hb, nchunk, HEAD_DIM, CHUNK), BF16),
                        pltpu.VMEM((hb, ts, CHUNK), BF16),
                        pltpu.VMEM((nchunk, 8, 128), F32)],
        compiler_params=_params("parallel", "parallel", "arbitrary"),
        name="gdn",
    )(proj, proj, proj, proj, proj, proj, proj, g, beta, conv_w, conv_w, conv_w, norm_w)


def _sc_kernel(a_ref, wb0_ref, wb1_ref, wc0_ref, wc1_ref, wh0_ref, wh1_ref, cw_ref, o_ref, p_ref, b_ref, *,
               row_off, rows_per_step, nstep):
    def proj(w0_ref, w1_ref):
        w_t = jnp.concatenate([w0_ref[row_off:, :], w1_ref[...]], axis=0)
        return lax.dot_general(a_ref[...], w_t, NT_DIMS, preferred_element_type=F32)

    p_ref[0:8, :] = jnp.zeros((8, p_ref.shape[1]), F32)
    p_ref[8:, :] = proj(wc0_ref, wc1_ref)
    p_ref[8:, :] *= proj(wh0_ref, wh1_ref)
    b_ref[...] = proj(wb0_ref, wb1_ref)
    w = cw_ref[...]

    def step(i, carry):
        r0 = pl.multiple_of(i * rows_per_step, rows_per_step)
        ext = p_ref[pl.ds(r0, rows_per_step + 8), :]
        acc = ext[8:, :] * w[SC_CONV - 1:SC_CONV, :]
        for t in range(SC_CONV - 1):
            off = 8 - (SC_CONV - 1) + t
            acc = acc + ext[off:off + rows_per_step, :] * w[t:t + 1, :]
        rows = pl.ds(r0, rows_per_step)
        o_ref[rows, :] = (b_ref[rows, :] * acc).astype(o_ref.dtype)
        return carry

    lax.fori_loop(0, nstep, step, 0)


def _sc_proj_conv(xn, w_ext_t, conv_w, *, seq, row_off, tc=256, rows_per_step=128):
    m, d = xn.shape
    nblk = SC_WIDTH // tc
    sub = tc // row_off

    def wrows(group):
        lead = pl.BlockSpec((tc, d), lambda b, j: (group * nblk + j, 0))
        tail = pl.BlockSpec((row_off, d), lambda b, j: ((group * nblk + j + 1) * sub, 0))
        return [lead, tail]

    return pl.pallas_call(
        functools.partial(_sc_kernel, row_off=row_off, rows_per_step=rows_per_step, nstep=seq // rows_per_step),
        grid=(m // seq, nblk),
        in_specs=[pl.BlockSpec((seq, d), lambda b, j: (b, 0), pipeline_mode=pl.Buffered(1)),
                  *wrows(0), *wrows(1), *wrows(2), pl.BlockSpec((SC_CONV, tc), lambda b, j: (0, j))],
        out_specs=pl.BlockSpec((seq, tc), lambda b, j: (b, j)),
        out_shape=jax.ShapeDtypeStruct((m, SC_WIDTH), BF16),
        scratch_shapes=[pltpu.VMEM((8 + seq, tc), F32), pltpu.VMEM((seq, tc), F32)],
        compiler_params=_params("parallel", "parallel"),
        name="sc_proj_conv",
    )(xn, *([w_ext_t] * 6), conv_w)


def _res_norm_kernel(t_ref, r_ref, gpost_ref, gnext_ref, h_ref, hn_ref):
    t = t_ref[...]
    h = r_ref[...] + t * lax.rsqrt(jnp.mean(t * t, axis=-1, keepdims=True) + NORM_EPS) * gpost_ref[...]
    h_ref[...] = h
    hn = h * lax.rsqrt(jnp.mean(h * h, axis=-1, keepdims=True) + NORM_EPS) * gnext_ref[...]
    hn_ref[...] = hn.astype(hn_ref.dtype)


def _res_norm(t, resid, g_post, g_next, tm=256):
    m, d = t.shape
    row = lambda i: (i, 0)
    fixed = lambda i: (0, 0)
    return pl.pallas_call(
        _res_norm_kernel,
        grid=(m // tm,),
        in_specs=[pl.BlockSpec((tm, d), row), pl.BlockSpec((tm, d), row),
                  pl.BlockSpec((1, d), fixed), pl.BlockSpec((1, d), fixed)],
        out_specs=[pl.BlockSpec((tm, d), row), pl.BlockSpec((tm, d), row)],
        out_shape=[jax.ShapeDtypeStruct((m, d), F32), jax.ShapeDtypeStruct((m, d), BF16)],
        compiler_params=_params("parallel"),
        name="res_norm",
    )(t, resid, g_post.reshape(1, d), g_next.reshape(1, d))


def _res_final_kernel(t_ref, r_ref, g_ref, o_ref):
    t = t_ref[...]
    o_ref[...] = r_ref[...] + t * lax.rsqrt(jnp.mean(t * t, axis=-1, keepdims=True) + NORM_EPS) * g_ref[...]


def _res_final(t, resid, g, tm=256):
    m, d = t.shape
    row = lambda i: (i, 0)
    return pl.pallas_call(
        _res_final_kernel,
        grid=(m // tm,),
        in_specs=[pl.BlockSpec((tm, d), row), pl.BlockSpec((tm, d), row), pl.BlockSpec((1, d), lambda i: (0, 0))],
        out_specs=pl.BlockSpec((tm, d), row),
        out_shape=jax.ShapeDtypeStruct((m, d), F32),
        compiler_params=_params("parallel"),
        name="res_final",
    )(t, resid, g.reshape(1, d))


def _layer(h, norm_mix_pre, w_in, conv_qkv_w, a_log, dt_bias, gdn_norm_w, conv_sc_w, w_out,
           norm_mix_post, norm_mlp_pre, w_up, w_down, norm_mlp_post, *, bsz, seq):
    m = bsz * seq
    n_gdn = 4 * GDN_WIDTH
    n_ab = 2 * GDN_HEADS

    xn = _rmsnorm(h, norm_mix_pre)

    w_in_t = w_in.T
    w_ext_t = w_in_t[n_gdn:].astype(BF16)
    lane_pad = jnp.zeros((1, 128 - GDN_HEADS), F32)
    a_log_p = jnp.concatenate([a_log.reshape(1, -1), lane_pad], axis=1)
    dt_bias_p = jnp.concatenate([dt_bias.reshape(1, -1), lane_pad], axis=1)

    proj = _matmul(xn, w_in_t, tm=2048, tn=512, tk=D_MODEL, out_dtype=F32, n_out=n_gdn, w_rows_are_outputs=True,
                   name="in_proj")
    g, beta = _gate_proj(xn, w_in_t, a_log_p, dt_bias_p, row0=n_gdn)

    proj3 = proj.reshape(bsz, seq, -1)
    gdn_out = _gdn(proj3, g.reshape(bsz, seq, 128), beta.reshape(bsz, seq, 128), conv_qkv_w,
                   gdn_norm_w.reshape(1, HEAD_DIM), bsz=bsz, seq=seq)
    sc_out = _sc_proj_conv(xn, w_ext_t, conv_sc_w, seq=seq, row_off=n_ab)

    mix = _matmul_pair(gdn_out.reshape(m, -1), sc_out, w_out, tm=2048, tn=512,
                       out_dtype=F32, name="out_proj")
    h, hn = _res_norm(mix, h, norm_mix_post, norm_mlp_pre)

    hid = _matmul(hn, w_up, tm=2048, tn=512, tk=D_MODEL, out_dtype=BF16, act="relu2", name="mlp_up")
    ff = _matmul(hid, w_down, tm=2048, tn=1024, tk=2048, out_dtype=F32, name="mlp_down")
    return _res_final(ff, h, norm_mlp_post)


def kernel(x, norm_mix_pre, w_in, conv_qkv_w, a_log, dt_bias, gdn_norm_w, conv_sc_w, w_out, norm_mix_post,
           norm_mlp_pre, w_up, w_down, norm_mlp_post):
    bsz, seq, d = x.shape
    h = x.reshape(bsz * seq, d)
    for l in range(norm_mix_pre.shape[0]):
        h = _layer(h, norm_mix_pre[l], w_in[l], conv_qkv_w[l], a_log[l], dt_bias[l], gdn_norm_w[l], conv_sc_w[l],
                   w_out[l], norm_mix_post[l], norm_mlp_pre[l], w_up[l], w_down[l], norm_mlp_post[l],
                   bsz=bsz, seq=seq)
    return h.reshape(bsz, seq, d)
```

```python
import functools

import jax
import jax.numpy as jnp
from jax import lax
from jax.experimental import pallas as pl
from jax.experimental.pallas import tpu as pltpu

F32 = jnp.float32
BF16 = jnp.bfloat16

D_MODEL = 4096
CHUNK = 128
HEAD_DIM = 128
GDN_WIDTH = 2048
GDN_HEADS = GDN_WIDTH // HEAD_DIM
SC_WIDTH = 2048
GDN_CONV = 4
SC_CONV = 3
NORM_EPS = 1e-6
L2_EPS = 1e-6

V7X_VMEM_LIMIT_BYTES = 60 * 1024 * 1024

NT_DIMS = (((1,), (1,)), ((), ()))


def _params(*sem):
    return pltpu.CompilerParams(dimension_semantics=sem, vmem_limit_bytes=V7X_VMEM_LIMIT_BYTES)


def _sigmoid(x):
    return 1.0 / (1.0 + jnp.exp(-x))


def _silu(x):
    half = 0.5 * x
    return half + half * jnp.tanh(half)


def _rmsnorm_kernel(x_ref, g_ref, o_ref):
    x = x_ref[...]
    ms = jnp.mean(x * x, axis=-1, keepdims=True)
    o_ref[...] = (x * lax.rsqrt(ms + NORM_EPS) * g_ref[...]).astype(o_ref.dtype)


def _rmsnorm(x, g, tm=512):
    m, d = x.shape
    return pl.pallas_call(
        _rmsnorm_kernel,
        grid=(m // tm,),
        in_specs=[pl.BlockSpec((tm, d), lambda i: (i, 0)), pl.BlockSpec((1, d), lambda i: (0, 0))],
        out_specs=pl.BlockSpec((tm, d), lambda i: (i, 0)),
        out_shape=jax.ShapeDtypeStruct((m, d), BF16),
        compiler_params=_params("parallel"),
        name="rmsnorm",
    )(x, g.reshape(1, d))


def _mm_kernel(a_ref, w_ref, o_ref, *, nk, act, w_rows_are_outputs):
    def part():
        w = w_ref[...].astype(BF16)
        if w_rows_are_outputs:
            return lax.dot_general(a_ref[...], w, NT_DIMS, preferred_element_type=F32)
        return jnp.dot(a_ref[...], w, preferred_element_type=F32)

    if nk == 1:
        acc = part()
        if act == "relu2":
            r = jnp.maximum(acc, 0.0)
            acc = r * r
        o_ref[...] = acc.astype(o_ref.dtype)
        return
    k = pl.program_id(2)

    @pl.when(k == 0)
    def _():
        o_ref[...] = part()

    @pl.when(k > 0)
    def _():
        o_ref[...] += part()


def _a_buffering(nk):
    return pl.Buffered(1) if nk == 1 else None


def _matmul(a, w, *, tm, tn, tk, out_dtype, n_out=None, act=None, w_rows_are_outputs=False, name="matmul"):
    m, kdim = a.shape
    if n_out is None:
        n_out = w.shape[0] if w_rows_are_outputs else w.shape[1]
    n = n_out
    nk = kdim // tk
    assert nk == 1 or (act is None and out_dtype == F32)
    if w_rows_are_outputs:
        w_spec = pl.BlockSpec((tn, tk), lambda i, j, k: (j, k))
    else:
        w_spec = pl.BlockSpec((tk, tn), lambda i, j, k: (k, j))
    return pl.pallas_call(
        functools.partial(_mm_kernel, nk=nk, act=act, w_rows_are_outputs=w_rows_are_outputs),
        grid=(m // tm, n // tn, nk),
        in_specs=[pl.BlockSpec((tm, tk), lambda i, j, k: (i, k), pipeline_mode=_a_buffering(nk)), w_spec],
        out_specs=pl.BlockSpec((tm, tn), lambda i, j, k: (i, j)),
        out_shape=jax.ShapeDtypeStruct((m, n), out_dtype),
        compiler_params=_params("parallel", "parallel", "arbitrary"),
        name=name,
    )(a, w)


def _mm_pair_kernel(a1_ref, a2_ref, w1_ref, w2_ref, o_ref):
    acc = jnp.dot(a1_ref[...], w1_ref[...].astype(BF16), preferred_element_type=F32)
    acc = acc + jnp.dot(a2_ref[...], w2_ref[...].astype(BF16), preferred_element_type=F32)
    o_ref[...] = acc.astype(o_ref.dtype)


def _matmul_pair(a1, a2, w, *, tm, tn, out_dtype, name):
    m, k1 = a1.shape
    _, k2 = a2.shape
    assert k1 == k2 and w.shape[0] == k1 + k2
    n = w.shape[1]
    return pl.pallas_call(
        _mm_pair_kernel,
        grid=(m // tm, n // tn),
        in_specs=[pl.BlockSpec((tm, k1), lambda i, j: (i, 0), pipeline_mode=pl.Buffered(1)),
                  pl.BlockSpec((tm, k2), lambda i, j: (i, 0), pipeline_mode=pl.Buffered(1)),
                  pl.BlockSpec((k1, tn), lambda i, j: (0, j)), pl.BlockSpec((k2, tn), lambda i, j: (1, j))],
        out_specs=pl.BlockSpec((tm, tn), lambda i, j: (i, j)),
        out_shape=jax.ShapeDtypeStruct((m, n), out_dtype),
        compiler_params=_params("parallel", "parallel"),
        name=name,
    )(a1, a2, w, w)


def _gdn_kernel(q_ref, k_ref, v_ref, z_ref, qh_ref, kh_ref, vh_ref, g_ref, b_ref, wq_ref, wk_ref, wv_ref, nw_ref,
                o_ref, s_ref, u_ref, wqd_ref, kdt_ref, qkm_ref, egl_ref, *, hb, nchunk):
    head0 = pl.program_id(1) * hb
    tile = pl.program_id(2)

    @pl.when(tile == 0)
    def _():
        s_ref[...] = jnp.zeros_like(s_ref)

    row = lax.broadcasted_iota(jnp.int32, (CHUNK, CHUNK), 0)
    col = lax.broadcasted_iota(jnp.int32, (CHUNK, CHUNK), 1)
    tri_incl = row >= col
    tri_strict = row > col
    tri_f = tri_incl.astype(F32)
    eye = (row == col).astype(F32)
    shift = (128 - head0) % 128
    nt = (((1,), (1,)), ((), ()))
    keep_halo = jnp.where(tile > 0, 1.0, 0.0)

    heads = range(hb)
    head_lanes = [slice(j * HEAD_DIM, (j + 1) * HEAD_DIM) for j in heads]

    def precompute(c, first_in_tile=False):
        r0 = pl.multiple_of(c * CHUNK, CHUNK)
        rows = pl.ds(r0, CHUNK)

        graw = pltpu.roll(g_ref[0, rows, :], shift, axis=1)
        beta = pltpu.roll(b_ref[0, rows, :], shift, axis=1)
        gc = jnp.dot(tri_f, graw, preferred_element_type=F32, precision=lax.Precision.HIGHEST)
        eg = jnp.exp(gc)
        g_last = gc[CHUNK - 1:CHUNK, :]
        eg_rest = jnp.exp(g_last - gc)
        gc_t = gc.T

        def conv_silu(ref, halo_ref, w_ref, lanes):
            cur = ref[0, rows, lanes]
            w = w_ref[:, lanes]
            acc = cur * w[GDN_CONV - 1:GDN_CONV, :]
            if first_in_tile:
                prev = halo_ref[0, :, lanes] * keep_halo
            else:
                prev = ref[0, pl.ds(pl.multiple_of(r0 - 8, 8), 8), lanes]
            ext = jnp.concatenate([prev, cur], axis=0)
            for t in range(GDN_CONV - 1):
                back = GDN_CONV - 1 - t
                acc = acc + ext[8 - back:8 - back + CHUNK, :] * w[t:t + 1, :]
            return _silu(acc)

        q = [conv_silu(q_ref, qh_ref, wq_ref, head_lanes[j]) for j in heads]
        k = [conv_silu(k_ref, kh_ref, wk_ref, head_lanes[j]) for j in heads]
        v = [conv_silu(v_ref, vh_ref, wv_ref, head_lanes[j]) for j in heads]
        q = [x * (lax.rsqrt(jnp.sum(x * x, axis=-1, keepdims=True) + L2_EPS) * (HEAD_DIM ** -0.5)) for x in q]
        k = [x * lax.rsqrt(jnp.sum(x * x, axis=-1, keepdims=True) + L2_EPS) for x in k]
        kb = [x.astype(BF16) for x in k]
        kk = [lax.dot_general(kb[j], kb[j], nt, preferred_element_type=F32) for j in heads]
        qk = [lax.dot_general(q[j].astype(BF16), kb[j], nt, preferred_element_type=F32) for j in heads]
        yield

        beta_col = [beta[:, j:j + 1] for j in heads]
        decay = [jnp.where(tri_incl, jnp.exp(gc[:, j:j + 1] - gc_t[j:j + 1, :]), 0.0) for j in heads]
        low = [jnp.where(tri_strict, beta_col[j] * kk[j] * decay[j], 0.0) for j in heads]

        inv = [eye - x for x in low]
        pw = low
        for _ in range(CHUNK.bit_length() - 2):
            pwb = [x.astype(BF16) for x in pw]
            pw = [jnp.dot(x, x, preferred_element_type=F32) for x in pwb]
            inv = [inv[j] + jnp.dot(inv[j].astype(BF16), pw[j].astype(BF16), preferred_element_type=F32)
                   for j in heads]
            yield

        rhs = [jnp.concatenate([v[j] * beta_col[j], k[j] * (beta_col[j] * eg[:, j:j + 1])], axis=1) for j in heads]
        sol = [jnp.dot(inv[j].astype(BF16), rhs[j].astype(BF16), preferred_element_type=F32) for j in heads]
        yield
        egl_ref[c] = jnp.broadcast_to(jnp.exp(g_last), (8, 128))
        for j in heads:
            u_ref[j, rows, :] = sol[j][:, :HEAD_DIM]
            wqd_ref[j, c, :CHUNK, :] = sol[j][:, HEAD_DIM:].astype(BF16)
            wqd_ref[j, c, CHUNK:, :] = (q[j] * eg[:, j:j + 1]).astype(BF16)
            kdt_ref[j, c] = (k[j] * eg_rest[:, j:j + 1]).T.astype(BF16)
            qkm_ref[j, rows, :] = (qk[j] * decay[j]).astype(BF16)

    def scan(c):
        r0 = pl.multiple_of(c * CHUNK, CHUNK)
        rows = pl.ds(r0, CHUNK)
        egl = egl_ref[c]
        state = [s_ref[j] for j in heads]
        u = [u_ref[j, rows, :] for j in heads]
        kdt = [kdt_ref[j, c] for j in heads]
        qkm = [qkm_ref[j, rows, :] for j in heads]
        proj = [jnp.dot(wqd_ref[j, c], state[j].astype(BF16), preferred_element_type=F32) for j in heads]
        yield
        v_new = [(u[j] - proj[j][:CHUNK]).astype(BF16) for j in heads]
        for j in heads:
            s_ref[j] = state[j] * egl[0:1, j:j + 1] + jnp.dot(kdt[j], v_new[j], preferred_element_type=F32)
        o = [proj[j][CHUNK:] + jnp.dot(qkm[j], v_new[j], preferred_element_type=F32) for j in heads]
        yield
        for j in heads:
            ms = jnp.mean(o[j] * o[j], axis=-1, keepdims=True)
            z = z_ref[0, rows, head_lanes[j]]
            out = o[j] * lax.rsqrt(ms + NORM_EPS) * nw_ref[...] * _silu(z)
            o_ref[0, rows, head_lanes[j]] = out.astype(o_ref.dtype)

    def interleave(*parts):
        parts = list(parts)
        while parts:
            parts = [p for p in parts if next(p, StopIteration) is not StopIteration]

    def chain(*parts):
        for p in parts:
            yield from p

    assert nchunk % 2 == 0
    interleave(precompute(0, first_in_tile=True), precompute(1))

    def steady(i, carry):
        c = 2 * i
        interleave(precompute(c + 2), precompute(c + 3), chain(scan(c), scan(c + 1)))
        return carry

    lax.fori_loop(0, nchunk // 2 - 1, steady, 0)
    interleave(chain(scan(nchunk - 2), scan(nchunk - 1)))


def _gdn(proj, g, beta, conv_w, norm_w, *, bsz, seq, hb=8, ts=512):
    nblk = GDN_WIDTH // (HEAD_DIM * hb)
    wblk = HEAD_DIM * hb
    nchunk = ts // CHUNK

    def col(group):
        return pl.BlockSpec((1, ts, wblk), lambda b, h, t: (b, t, group * nblk + h))

    def halo(group):
        return pl.BlockSpec((1, 8, wblk), lambda b, h, t: (b, jnp.maximum(t * (ts // 8) - 1, 0), group * nblk + h))

    def wcol(group):
        return pl.BlockSpec((GDN_CONV, wblk), lambda b, h, t: (0, group * nblk + h))

    gspec = pl.BlockSpec((1, ts, 128), lambda b, h, t: (b, t, 0))
    return pl.pallas_call(
        functools.partial(_gdn_kernel, hb=hb, nchunk=nchunk),
        grid=(bsz, nblk, seq // ts),
        in_specs=[col(0), col(1), col(2), col(3), halo(0), halo(1), halo(2), gspec, gspec,
                  wcol(0), wcol(1), wcol(2), pl.BlockSpec((1, HEAD_DIM), lambda b, h, t: (0, 0))],
        out_specs=pl.BlockSpec((1, ts, wblk), lambda b, h, t: (b, t, h)),
        out_shape=jax.ShapeDtypeStruct((bsz, seq, GDN_WIDTH), BF16),
        scratch_shapes=[pltpu.VMEM((hb, HEAD_DIM, HEAD_DIM), F32),
                        pltpu.VMEM((hb, ts, HEAD_DIM), F32),
                        pltpu.VMEM((hb, nchunk, 2 * CHUNK, HEAD_DIM), BF16),
                        pltpu.VMEM((hb, nchunk, HEAD_DIM, CHUNK), BF16),
                        pltpu.VMEM((hb, ts, CHUNK), BF16),
                        pltpu.VMEM((nchunk, 8, 128), F32)],
        compiler_params=_params("parallel", "parallel", "arbitrary"),
        name="gdn",
    )(proj, proj, proj, proj, proj, proj, proj, g, beta, conv_w, conv_w, conv_w, norm_w)


def _sc_kernel(a_ref, wb0_ref, wb1_ref, wc0_ref, wc1_ref, wh0_ref, wh1_ref, cw_ref, wg_ref, alog_ref, dtb_ref,
               o_ref, g_ref, beta_ref, p_ref, b_ref, *, row_off, rows_per_step, nstep):
    @pl.when(pl.program_id(1) == 0)
    def _():
        p = lax.dot_general(a_ref[...], wg_ref[...], NT_DIMS, preferred_element_type=F32)
        a = p + dtb_ref[...]
        softplus = jnp.maximum(a, 0.0) + jnp.log(1.0 + jnp.exp(-jnp.abs(a)))
        g_ref[...] = -jnp.exp(alog_ref[...]) * softplus
        beta_ref[...] = _sigmoid(pltpu.roll(p, 128 - GDN_HEADS, axis=1))

    def proj(w0_ref, w1_ref):
        w_t = jnp.concatenate([w0_ref[row_off:, :], w1_ref[...]], axis=0)
        return lax.dot_general(a_ref[...], w_t, NT_DIMS, preferred_element_type=F32)

    p_ref[0:8, :] = jnp.zeros((8, p_ref.shape[1]), F32)
    p_ref[8:, :] = proj(wc0_ref, wc1_ref)
    p_ref[8:, :] *= proj(wh0_ref, wh1_ref)
    b_ref[...] = proj(wb0_ref, wb1_ref)
    w = cw_ref[...]

    def step(i, carry):
        r0 = pl.multiple_of(i * rows_per_step, rows_per_step)
        ext = p_ref[pl.ds(r0, rows_per_step + 8), :]
        acc = ext[8:, :] * w[SC_CONV - 1:SC_CONV, :]
        for t in range(SC_CONV - 1):
            off = 8 - (SC_CONV - 1) + t
            acc = acc + ext[off:off + rows_per_step, :] * w[t:t + 1, :]
        rows = pl.ds(r0, rows_per_step)
        o_ref[rows, :] = (b_ref[rows, :] * acc).astype(o_ref.dtype)
        return carry

    lax.fori_loop(0, nstep, step, 0)


def _sc_proj_conv(xn, w_ext_t, conv_w, a_log, dt_bias, *, seq, row_off, tc=256, rows_per_step=128):
    m, d = xn.shape
    nblk = SC_WIDTH // tc
    sub = tc // row_off

    def wrows(group):
        lead = pl.BlockSpec((tc, d), lambda b, j: (group * nblk + j, 0))
        tail = pl.BlockSpec((row_off, d), lambda b, j: ((group * nblk + j + 1) * sub, 0))
        return [lead, tail]

    return pl.pallas_call(
        functools.partial(_sc_kernel, row_off=row_off, rows_per_step=rows_per_step, nstep=seq // rows_per_step),
        grid=(m // seq, nblk),
        in_specs=[pl.BlockSpec((seq, d), lambda b, j: (b, 0), pipeline_mode=pl.Buffered(1)),
                  *wrows(0), *wrows(1), *wrows(2), pl.BlockSpec((SC_CONV, tc), lambda b, j: (0, j)),
                  pl.BlockSpec((128, d), lambda b, j: (0, 0)),
                  pl.BlockSpec((1, 128), lambda b, j: (0, 0)), pl.BlockSpec((1, 128), lambda b, j: (0, 0))],
        out_specs=[pl.BlockSpec((seq, tc), lambda b, j: (b, j)),
                   pl.BlockSpec((seq, 128), lambda b, j: (b, 0)), pl.BlockSpec((seq, 128), lambda b, j: (b, 0))],
        out_shape=[jax.ShapeDtypeStruct((m, SC_WIDTH), BF16),
                   jax.ShapeDtypeStruct((m, 128), F32), jax.ShapeDtypeStruct((m, 128), F32)],
        scratch_shapes=[pltpu.VMEM((8 + seq, tc), F32), pltpu.VMEM((seq, tc), F32)],
        compiler_params=_params("parallel", "arbitrary"),
        name="sc_proj_conv",
    )(xn, *([w_ext_t] * 6), conv_w, w_ext_t, a_log, dt_bias)


def _res_norm_kernel(t_ref, r_ref, gpost_ref, gnext_ref, h_ref, hn_ref):
    t = t_ref[...]
    h = r_ref[...] + t * lax.rsqrt(jnp.mean(t * t, axis=-1, keepdims=True) + NORM_EPS) * gpost_ref[...]
    h_ref[...] = h
    hn = h * lax.rsqrt(jnp.mean(h * h, axis=-1, keepdims=True) + NORM_EPS) * gnext_ref[...]
    hn_ref[...] = hn.astype(hn_ref.dtype)


def _res_norm(t, resid, g_post, g_next, tm=256):
    m, d = t.shape
    row = lambda i: (i, 0)
    fixed = lambda i: (0, 0)
    return pl.pallas_call(
        _res_norm_kernel,
        grid=(m // tm,),
        in_specs=[pl.BlockSpec((tm, d), row), pl.BlockSpec((tm, d), row),
                  pl.BlockSpec((1, d), fixed), pl.BlockSpec((1, d), fixed)],
        out_specs=[pl.BlockSpec((tm, d), row), pl.BlockSpec((tm, d), row)],
        out_shape=[jax.ShapeDtypeStruct((m, d), F32), jax.ShapeDtypeStruct((m, d), BF16)],
        compiler_params=_params("parallel"),
        name="res_norm",
    )(t, resid, g_post.reshape(1, d), g_next.reshape(1, d))


def _res_final_kernel(t_ref, r_ref, g_ref, o_ref):
    t = t_ref[...]
    o_ref[...] = r_ref[...] + t * lax.rsqrt(jnp.mean(t * t, axis=-1, keepdims=True) + NORM_EPS) * g_ref[...]


def _res_final(t, resid, g, tm=256):
    m, d = t.shape
    row = lambda i: (i, 0)
    return pl.pallas_call(
        _res_final_kernel,
        grid=(m // tm,),
        in_specs=[pl.BlockSpec((tm, d), row), pl.BlockSpec((tm, d), row), pl.BlockSpec((1, d), lambda i: (0, 0))],
        out_specs=pl.BlockSpec((tm, d), row),
        out_shape=jax.ShapeDtypeStruct((m, d), F32),
        compiler_params=_params("parallel"),
        name="res_final",
    )(t, resid, g.reshape(1, d))


def _layer(h, norm_mix_pre, w_in, conv_qkv_w, a_log, dt_bias, gdn_norm_w, conv_sc_w, w_out,
           norm_mix_post, norm_mlp_pre, w_up, w_down, norm_mlp_post, *, bsz, seq):
    m = bsz * seq
    n_gdn = 4 * GDN_WIDTH
    n_ab = 2 * GDN_HEADS

    xn = _rmsnorm(h, norm_mix_pre)

    w_in_t = w_in.T
    w_ext_t = w_in_t[n_gdn:].astype(BF16)
    lane_pad = jnp.zeros((1, 128 - GDN_HEADS), F32)
    a_log_p = jnp.concatenate([a_log.reshape(1, -1), lane_pad], axis=1)
    dt_bias_p = jnp.concatenate([dt_bias.reshape(1, -1), lane_pad], axis=1)

    proj = _matmul(xn, w_in_t, tm=2048, tn=512, tk=D_MODEL, out_dtype=F32, n_out=n_gdn, w_rows_are_outputs=True,
                   name="in_proj")
    sc_out, g, beta = _sc_proj_conv(xn, w_ext_t, conv_sc_w, a_log_p, dt_bias_p, seq=seq, row_off=n_ab)

    proj3 = proj.reshape(bsz, seq, -1)
    gdn_out = _gdn(proj3, g.reshape(bsz, seq, 128), beta.reshape(bsz, seq, 128), conv_qkv_w,
                   gdn_norm_w.reshape(1, HEAD_DIM), bsz=bsz, seq=seq)

    mix = _matmul_pair(gdn_out.reshape(m, -1), sc_out, w_out, tm=2048, tn=512,
                       out_dtype=F32, name="out_proj")
    h, hn = _res_norm(mix, h, norm_mix_post, norm_mlp_pre)

    hid = _matmul(hn, w_up, tm=2048, tn=512, tk=D_MODEL, out_dtype=BF16, act="relu2", name="mlp_up")
    ff = _matmul(hid, w_down, tm=2048, tn=1024, tk=2048, out_dtype=F32, name="mlp_down")
    return _res_final(ff, h, norm_mlp_post)


def kernel(x, norm_mix_pre, w_in, conv_qkv_w, a_log, dt_bias, gdn_norm_w, conv_sc_w, w_out, norm_mix_post,
           norm_mlp_pre, w_up, w_down, norm_mlp_post):
    bsz, seq, d = x.shape
    h = x.reshape(bsz * seq, d)
    for l in range(norm_mix_pre.shape[0]):
        h = _layer(h, norm_mix_pre[l], w_in[l], conv_qkv_w[l], a_log[l], dt_bias[l], gdn_norm_w[l], conv_sc_w[l],
                   w_out[l], norm_mix_post[l], norm_mlp_pre[l], w_up[l], w_down[l], norm_mlp_post[l],
                   bsz=bsz, seq=seq)
    return h.reshape(bsz, seq, d)
```

```python
import functools

import jax
import jax.numpy as jnp
from jax import lax
from jax.experimental import pallas as pl
from jax.experimental.pallas import tpu as pltpu

F32 = jnp.float32
BF16 = jnp.bfloat16

D_MODEL = 4096
CHUNK = 128
HEAD_DIM = 128
GDN_WIDTH = 2048
GDN_HEADS = GDN_WIDTH // HEAD_DIM
SC_WIDTH = 2048
GDN_CONV = 4
SC_CONV = 3
NORM_EPS = 1e-6
L2_EPS = 1e-6

V7X_VMEM_LIMIT_BYTES = 60 * 1024 * 1024

NT_DIMS = (((1,), (1,)), ((), ()))


def _params(*sem):
    return pltpu.CompilerParams(dimension_semantics=sem, vmem_limit_bytes=V7X_VMEM_LIMIT_BYTES)


def _sigmoid(x):
    return 1.0 / (1.0 + jnp.exp(-x))


def _silu(x):
    half = 0.5 * x
    return half + half * jnp.tanh(half)


def _rmsnorm_kernel(x_ref, g_ref, o_ref):
    x = x_ref[...]
    ms = jnp.mean(x * x, axis=-1, keepdims=True)
    o_ref[...] = (x * lax.rsqrt(ms + NORM_EPS) * g_ref[...]).astype(o_ref.dtype)


def _rmsnorm(x, g, tm=512):
    m, d = x.shape
    return pl.pallas_call(
        _rmsnorm_kernel,
        grid=(m // tm,),
        in_specs=[pl.BlockSpec((tm, d), lambda i: (i, 0)), pl.BlockSpec((1, d), lambda i: (0, 0))],
        out_specs=pl.BlockSpec((tm, d), lambda i: (i, 0)),
        out_shape=jax.ShapeDtypeStruct((m, d), BF16),
        compiler_params=_params("parallel"),
        name="rmsnorm",
    )(x, g.reshape(1, d))


def _mm_kernel(a_ref, w_ref, *refs, nk, act, w_rows_are_outputs, with_cast):
    if with_cast:
        side_ref, o_ref, side_out_ref = refs
        side_out_ref[...] = side_ref[...].astype(side_out_ref.dtype)
    else:
        (o_ref,) = refs

    def part():
        w = w_ref[...].astype(BF16)
        if w_rows_are_outputs:
            return lax.dot_general(a_ref[...], w, NT_DIMS, preferred_element_type=F32)
        return jnp.dot(a_ref[...], w, preferred_element_type=F32)

    if nk == 1:
        acc = part()
        if act == "relu2":
            r = jnp.maximum(acc, 0.0)
            acc = r * r
        o_ref[...] = acc.astype(o_ref.dtype)
        return
    k = pl.program_id(2)

    @pl.when(k == 0)
    def _():
        o_ref[...] = part()

    @pl.when(k > 0)
    def _():
        o_ref[...] += part()


def _a_buffering(nk):
    return pl.Buffered(1) if nk == 1 else None


def _matmul(a, w, *, tm, tn, tk, out_dtype, n_out=None, act=None, w_rows_are_outputs=False, cast_rows=None,
            name="matmul"):
    m, kdim = a.shape
    if n_out is None:
        n_out = w.shape[0] if w_rows_are_outputs else w.shape[1]
    n = n_out
    nk = kdim // tk
    assert nk == 1 or (act is None and out_dtype == F32)
    if w_rows_are_outputs:
        w_spec = pl.BlockSpec((tn, tk), lambda i, j, k: (j, k))
    else:
        w_spec = pl.BlockSpec((tk, tn), lambda i, j, k: (k, j))
    grid = (m // tm, n // tn, nk)
    in_specs = [pl.BlockSpec((tm, tk), lambda i, j, k: (i, k), pipeline_mode=_a_buffering(nk)), w_spec]
    out_specs = [pl.BlockSpec((tm, tn), lambda i, j, k: (i, j))]
    out_shape = [jax.ShapeDtypeStruct((m, n), out_dtype)]
    operands = [a, w]
    if cast_rows is not None:
        row0, nrows, slab = cast_rows
        nslab = nrows // slab
        assert nrows % slab == 0 and row0 % slab == 0 and nslab <= grid[0] * grid[1] * grid[2]

        def slab_of(i, j, k):
            return jnp.minimum((i * grid[1] + j) * grid[2] + k, nslab - 1)

        in_specs.append(pl.BlockSpec((slab, w.shape[1]), lambda i, j, k: (row0 // slab + slab_of(i, j, k), 0)))
        out_specs.append(pl.BlockSpec((slab, w.shape[1]), lambda i, j, k: (slab_of(i, j, k), 0)))
        out_shape.append(jax.ShapeDtypeStruct((nrows, w.shape[1]), BF16))
        operands.append(w)
    out = pl.pallas_call(
        functools.partial(_mm_kernel, nk=nk, act=act, w_rows_are_outputs=w_rows_are_outputs,
                          with_cast=cast_rows is not None),
        grid=grid,
        in_specs=in_specs,
        out_specs=out_specs,
        out_shape=out_shape,
        compiler_params=_params(*(("arbitrary",) * 3 if cast_rows is not None else ("parallel", "parallel", "arbitrary"))),
        name=name,
    )(*operands)
    return out if cast_rows is not None else out[0]


def _mm_pair_kernel(a1_ref, a2_ref, w1_ref, w2_ref, o_ref):
    acc = jnp.dot(a1_ref[...], w1_ref[...].astype(BF16), preferred_element_type=F32)
    acc = acc + jnp.dot(a2_ref[...], w2_ref[...].astype(BF16), preferred_element_type=F32)
    o_ref[...] = acc.astype(o_ref.dtype)


def _matmul_pair(a1, a2, w, *, tm, tn, out_dtype, name):
    m, k1 = a1.shape
    _, k2 = a2.shape
    assert k1 == k2 and w.shape[0] == k1 + k2
    n = w.shape[1]
    return pl.pallas_call(
        _mm_pair_kernel,
        grid=(m // tm, n // tn),
        in_specs=[pl.BlockSpec((tm, k1), lambda i, j: (i, 0), pipeline_mode=pl.Buffered(1)),
                  pl.BlockSpec((tm, k2), lambda i, j: (i, 0), pipeline_mode=pl.Buffered(1)),
                  pl.BlockSpec((k1, tn), lambda i, j: (0, j)), pl.BlockSpec((k2, tn), lambda i, j: (1, j))],
        out_specs=pl.BlockSpec((tm, tn), lambda i, j: (i, j)),
        out_shape=jax.ShapeDtypeStruct((m, n), out_dtype),
        compiler_params=_params("parallel", "parallel"),
        name=name,
    )(a1, a2, w, w)


def _gdn_kernel(q_ref, k_ref, v_ref, z_ref, qh_ref, kh_ref, vh_ref, g_ref, b_ref, wq_ref, wk_ref, wv_ref, nw_ref,
                o_ref, s_ref, u_ref, wqd_ref, kdt_ref, qkm_ref, egl_ref, *, hb, nchunk):
    head0 = pl.program_id(1) * hb
    tile = pl.program_id(2)

    @pl.when(tile == 0)
    def _():
        s_ref[...] = jnp.zeros_like(s_ref)

    row = lax.broadcasted_iota(jnp.int32, (CHUNK, CHUNK), 0)
    col = lax.broadcasted_iota(jnp.int32, (CHUNK, CHUNK), 1)
    tri_incl = row >= col
    tri_strict = row > col
    tri_f = tri_incl.astype(F32)
    eye = (row == col).astype(F32)
    shift = (128 - head0) % 128
    nt = (((1,), (1,)), ((), ()))
    keep_halo = jnp.where(tile > 0, 1.0, 0.0)

    heads = range(hb)
    head_lanes = [slice(j * HEAD_DIM, (j + 1) * HEAD_DIM) for j in heads]

    def precompute(c, first_in_tile=False):
        r0 = pl.multiple_of(c * CHUNK, CHUNK)
        rows = pl.ds(r0, CHUNK)

        graw = pltpu.roll(g_ref[0, rows, :], shift, axis=1)
        beta = pltpu.roll(b_ref[0, rows, :], shift, axis=1)
        gc = jnp.dot(tri_f, graw, preferred_element_type=F32, precision=lax.Precision.HIGHEST)
        eg = jnp.exp(gc)
        g_last = gc[CHUNK - 1:CHUNK, :]
        eg_rest = jnp.exp(g_last - gc)
        gc_t = gc.T

        def conv_silu(ref, halo_ref, w_ref, lanes):
            cur = ref[0, rows, lanes]
            w = w_ref[:, lanes]
            acc = cur * w[GDN_CONV - 1:GDN_CONV, :]
            if first_in_tile:
                prev = halo_ref[0, :, lanes] * keep_halo
            else:
                prev = ref[0, pl.ds(pl.multiple_of(r0 - 8, 8), 8), lanes]
            ext = jnp.concatenate([prev, cur], axis=0)
            for t in range(GDN_CONV - 1):
                back = GDN_CONV - 1 - t
                acc = acc + ext[8 - back:8 - back + CHUNK, :] * w[t:t + 1, :]
            return _silu(acc)

        q = [conv_silu(q_ref, qh_ref, wq_ref, head_lanes[j]) for j in heads]
        k = [conv_silu(k_ref, kh_ref, wk_ref, head_lanes[j]) for j in heads]
        v = [conv_silu(v_ref, vh_ref, wv_ref, head_lanes[j]) for j in heads]
        q = [x * (lax.rsqrt(jnp.sum(x * x, axis=-1, keepdims=True) + L2_EPS) * (HEAD_DIM ** -0.5)) for x in q]
        k = [x * lax.rsqrt(jnp.sum(x * x, axis=-1, keepdims=True) + L2_EPS) for x in k]
        kb = [x.astype(BF16) for x in k]
        kk = [lax.dot_general(kb[j], kb[j], nt, preferred_element_type=F32) for j in heads]
        qk = [lax.dot_general(q[j].astype(BF16), kb[j], nt, preferred_element_type=F32) for j in heads]
        yield

        beta_col = [beta[:, j:j + 1] for j in heads]
        decay = [jnp.where(tri_incl, jnp.exp(gc[:, j:j + 1] - gc_t[j:j + 1, :]), 0.0) for j in heads]
        low = [jnp.where(tri_strict, beta_col[j] * kk[j] * decay[j], 0.0) for j in heads]

        inv = [eye - x for x in low]
        pw = low
        for _ in range(CHUNK.bit_length() - 2):
            pwb = [x.astype(BF16) for x in pw]
            pw = [jnp.dot(x, x, preferred_element_type=F32) for x in pwb]
            inv = [inv[j] + jnp.dot(inv[j].astype(BF16), pw[j].astype(BF16), preferred_element_type=F32)
                   for j in heads]
            yield

        rhs = [jnp.concatenate([v[j] * beta_col[j], k[j] * (beta_col[j] * eg[:, j:j + 1])], axis=1) for j in heads]
        sol = [jnp.dot(inv[j].astype(BF16), rhs[j].astype(BF16), preferred_element_type=F32) for j in heads]
        yield
        egl_ref[c] = jnp.broadcast_to(jnp.exp(g_last), (8, 128))
        for j in heads:
            u_ref[j, rows, :] = sol[j][:, :HEAD_DIM]
            wqd_ref[j, c, :CHUNK, :] = sol[j][:, HEAD_DIM:].astype(BF16)
            wqd_ref[j, c, CHUNK:, :] = (q[j] * eg[:, j:j + 1]).astype(BF16)
            kdt_ref[j, c] = (k[j] * eg_rest[:, j:j + 1]).T.astype(BF16)
            qkm_ref[j, rows, :] = (qk[j] * decay[j]).astype(BF16)

    def scan(c):
        r0 = pl.multiple_of(c * CHUNK, CHUNK)
        rows = pl.ds(r0, CHUNK)
        egl = egl_ref[c]
        state = [s_ref[j] for j in heads]
        u = [u_ref[j, rows, :] for j in heads]
        kdt = [kdt_ref[j, c] for j in heads]
        qkm = [qkm_ref[j, rows, :] for j in heads]
        proj = [jnp.dot(wqd_ref[j, c], state[j].astype(BF16), preferred_element_type=F32) for j in heads]
        yield
        v_new = [(u[j] - proj[j][:CHUNK]).astype(BF16) for j in heads]
        for j in heads:
            s_ref[j] = state[j] * egl[0:1, j:j + 1] + jnp.dot(kdt[j], v_new[j], preferred_element_type=F32)
        o = [proj[j][CHUNK:] + jnp.dot(qkm[j], v_new[j], preferred_element_type=F32) for j in heads]
        yield
        for j in heads:
            ms = jnp.mean(o[j] * o[j], axis=-1, keepdims=True)
            z = z_ref[0, rows, head_lanes[j]]
            out = o[j] * lax.rsqrt(ms + NORM_EPS) * nw_ref[...] * _silu(z)
            o_ref[0, rows, head_lanes[j]] = out.astype(o_ref.dtype)

    def interleave(*parts):
        parts = list(parts)
        while parts:
            parts = [p for p in parts if next(p, StopIteration) is not StopIteration]

    def chain(*parts):
        for p in parts:
            yield from p

    assert nchunk % 2 == 0
    interleave(precompute(0, first_in_tile=True), precompute(1))

    def steady(i, carry):
        c = 2 * i
        interleave(precompute(c + 2), precompute(c + 3), chain(scan(c), scan(c + 1)))
        return carry

    lax.fori_loop(0, nchunk // 2 - 1, steady, 0)
    interleave(chain(scan(nchunk - 2), scan(nchunk - 1)))


def _gdn(proj, g, beta, conv_w, norm_w, *, bsz, seq, hb=8, ts=512):
    nblk = GDN_WIDTH // (HEAD_DIM * hb)
    wblk = HEAD_DIM * hb
    nchunk = ts // CHUNK

    def col(group):
        return pl.BlockSpec((1, ts, wblk), lambda b, h, t: (b, t, group * nblk + h))

    def halo(group):
        return pl.BlockSpec((1, 8, wblk), lambda b, h, t: (b, jnp.maximum(t * (ts // 8) - 1, 0), group * nblk + h))

    def wcol(group):
        return pl.BlockSpec((GDN_CONV, wblk), lambda b, h, t: (0, group * nblk + h))

    gspec = pl.BlockSpec((1, ts, 128), lambda b, h, t: (b, t, 0))
    return pl.pallas_call(
        functools.partial(_gdn_kernel, hb=hb, nchunk=nchunk),
        grid=(bsz, nblk, seq // ts),
        in_specs=[col(0), col(1), col(2), col(3), halo(0), halo(1), halo(2), gspec, gspec,
                  wcol(0), wcol(1), wcol(2), pl.BlockSpec((1, HEAD_DIM), lambda b, h, t: (0, 0))],
        out_specs=pl.BlockSpec((1, ts, wblk), lambda b, h, t: (b, t, h)),
        out_shape=jax.ShapeDtypeStruct((bsz, seq, GDN_WIDTH), BF16),
        scratch_shapes=[pltpu.VMEM((hb, HEAD_DIM, HEAD_DIM), F32),
                        pltpu.VMEM((hb, ts, HEAD_DIM), F32),
                        pltpu.VMEM((hb, nchunk, 2 * CHUNK, HEAD_DIM), BF16),
                        pltpu.VMEM((hb, nchunk, HEAD_DIM, CHUNK), BF16),
                        pltpu.VMEM((hb, ts, CHUNK), BF16),
                        pltpu.VMEM((nchunk, 8, 128), F32)],
        compiler_params=_params("parallel", "parallel", "arbitrary"),
        name="gdn",
    )(proj, proj, proj, proj, proj, proj, proj, g, beta, conv_w, conv_w, conv_w, norm_w)


def _sc_kernel(a_ref, wb0_ref, wb1_ref, wc0_ref, wc1_ref, wh0_ref, wh1_ref, cw_ref, wg_ref, alog_ref, dtb_ref,
               o_ref, g_ref, beta_ref, p_ref, b_ref, *, row_off, rows_per_step, nstep):
    @pl.when(pl.program_id(1) == 0)
    def _():
        p = lax.dot_general(a_ref[...], wg_ref[...], NT_DIMS, preferred_element_type=F32)
        a = p + dtb_ref[...]
        softplus = jnp.maximum(a, 0.0) + jnp.log(1.0 + jnp.exp(-jnp.abs(a)))
        g_ref[...] = -jnp.exp(alog_ref[...]) * softplus
        beta_ref[...] = _sigmoid(pltpu.roll(p, 128 - GDN_HEADS, axis=1))

    def proj(w0_ref, w1_ref):
        w_t = jnp.concatenate([w0_ref[row_off:, :], w1_ref[...].astype(BF16)], axis=0)
        return lax.dot_general(a_ref[...], w_t, NT_DIMS, preferred_element_type=F32)

    p_ref[0:8, :] = jnp.zeros((8, p_ref.shape[1]), F32)
    p_ref[8:, :] = proj(wc0_ref, wc1_ref)
    p_ref[8:, :] *= proj(wh0_ref, wh1_ref)
    b_ref[...] = proj(wb0_ref, wb1_ref)
    w = cw_ref[...]

    def step(i, carry):
        r0 = pl.multiple_of(i * rows_per_step, rows_per_step)
        ext = p_ref[pl.ds(r0, rows_per_step + 8), :]
        acc = ext[8:, :] * w[SC_CONV - 1:SC_CONV, :]
        for t in range(SC_CONV - 1):
            off = 8 - (SC_CONV - 1) + t
            acc = acc + ext[off:off + rows_per_step, :] * w[t:t + 1, :]
        rows = pl.ds(r0, rows_per_step)
        o_ref[rows, :] = (b_ref[rows, :] * acc).astype(o_ref.dtype)
        return carry

    lax.fori_loop(0, nstep, step, 0)


def _sc_proj_conv(xn, w_tail_b, w_t, conv_w, a_log, dt_bias, *, seq, row0, row_off, tc=256, rows_per_step=128):
    m, d = xn.shape
    nblk = SC_WIDTH // tc
    sub = tc // row_off
    assert w_tail_b.shape[0] == 3 * SC_WIDTH and row0 % row_off == 0

    def wrows(group):
        lead = pl.BlockSpec((tc, d), lambda b, j: (group * nblk + j, 0))
        tail = pl.BlockSpec((row_off, d), lambda b, j: (row0 // row_off + (group * nblk + j + 1) * sub, 0))
        return [lead, tail]

    return pl.pallas_call(
        functools.partial(_sc_kernel, row_off=row_off, rows_per_step=rows_per_step, nstep=seq // rows_per_step),
        grid=(m // seq, nblk),
        in_specs=[pl.BlockSpec((seq, d), lambda b, j: (b, 0), pipeline_mode=pl.Buffered(1)),
                  *wrows(0), *wrows(1), *wrows(2), pl.BlockSpec((SC_CONV, tc), lambda b, j: (0, j)),
                  pl.BlockSpec((128, d), lambda b, j: (0, 0)),
                  pl.BlockSpec((1, 128), lambda b, j: (0, 0)), pl.BlockSpec((1, 128), lambda b, j: (0, 0))],
        out_specs=[pl.BlockSpec((seq, tc), lambda b, j: (b, j)),
                   pl.BlockSpec((seq, 128), lambda b, j: (b, 0)), pl.BlockSpec((seq, 128), lambda b, j: (b, 0))],
        out_shape=[jax.ShapeDtypeStruct((m, SC_WIDTH), BF16),
                   jax.ShapeDtypeStruct((m, 128), F32), jax.ShapeDtypeStruct((m, 128), F32)],
        scratch_shapes=[pltpu.VMEM((8 + seq, tc), F32), pltpu.VMEM((seq, tc), F32)],
        compiler_params=_params("parallel", "arbitrary"),
        name="sc_proj_conv",
    )(xn, *([w_tail_b, w_t] * 3), conv_w, w_tail_b, a_log, dt_bias)


def _res_norm_kernel(t_ref, r_ref, gpost_ref, gnext_ref, h_ref, hn_ref):
    t = t_ref[...]
    h = r_ref[...] + t * lax.rsqrt(jnp.mean(t * t, axis=-1, keepdims=True) + NORM_EPS) * gpost_ref[...]
    h_ref[...] = h
    hn = h * lax.rsqrt(jnp.mean(h * h, axis=-1, keepdims=True) + NORM_EPS) * gnext_ref[...]
    hn_ref[...] = hn.astype(hn_ref.dtype)


def _res_norm(t, resid, g_post, g_next, tm=256):
    m, d = t.shape
    row = lambda i: (i, 0)
    fixed = lambda i: (0, 0)
    return pl.pallas_call(
        _res_norm_kernel,
        grid=(m // tm,),
        in_specs=[pl.BlockSpec((tm, d), row), pl.BlockSpec((tm, d), row),
                  pl.BlockSpec((1, d), fixed), pl.BlockSpec((1, d), fixed)],
        out_specs=[pl.BlockSpec((tm, d), row), pl.BlockSpec((tm, d), row)],
        out_shape=[jax.ShapeDtypeStruct((m, d), F32), jax.ShapeDtypeStruct((m, d), BF16)],
        compiler_params=_params("parallel"),
        name="res_norm",
    )(t, resid, g_post.reshape(1, d), g_next.reshape(1, d))


def _res_final_kernel(t_ref, r_ref, g_ref, o_ref):
    t = t_ref[...]
    o_ref[...] = r_ref[...] + t * lax.rsqrt(jnp.mean(t * t, axis=-1, keepdims=True) + NORM_EPS) * g_ref[...]


def _res_final(t, resid, g, tm=256):
    m, d = t.shape
    row = lambda i: (i, 0)
    return pl.pallas_call(
        _res_final_kernel,
        grid=(m // tm,),
        in_specs=[pl.BlockSpec((tm, d), row), pl.BlockSpec((tm, d), row), pl.BlockSpec((1, d), lambda i: (0, 0))],
        out_specs=pl.BlockSpec((tm, d), row),
        out_shape=jax.ShapeDtypeStruct((m, d), F32),
        compiler_params=_params("parallel"),
        name="res_final",
    )(t, resid, g.reshape(1, d))


def _layer(h, norm_mix_pre, w_in, conv_qkv_w, a_log, dt_bias, gdn_norm_w, conv_sc_w, w_out,
           norm_mix_post, norm_mlp_pre, w_up, w_down, norm_mlp_post, *, bsz, seq):
    m = bsz * seq
    n_gdn = 4 * GDN_WIDTH
    n_ab = 2 * GDN_HEADS

    xn = _rmsnorm(h, norm_mix_pre)

    w_in_t = w_in.T
    lane_pad = jnp.zeros((1, 128 - GDN_HEADS), F32)
    a_log_p = jnp.concatenate([a_log.reshape(1, -1), lane_pad], axis=1)
    dt_bias_p = jnp.concatenate([dt_bias.reshape(1, -1), lane_pad], axis=1)

    proj, w_tail_b = _matmul(xn, w_in_t, tm=2048, tn=512, tk=D_MODEL, out_dtype=F32, n_out=n_gdn,
                             w_rows_are_outputs=True, cast_rows=(n_gdn, 3 * SC_WIDTH, 128), name="in_proj")
    sc_out, g, beta = _sc_proj_conv(xn, w_tail_b, w_in_t, conv_sc_w, a_log_p, dt_bias_p, seq=seq, row0=n_gdn,
                                    row_off=n_ab)

    proj3 = proj.reshape(bsz, seq, -1)
    gdn_out = _gdn(proj3, g.reshape(bsz, seq, 128), beta.reshape(bsz, seq, 128), conv_qkv_w,
                   gdn_norm_w.reshape(1, HEAD_DIM), bsz=bsz, seq=seq)

    mix = _matmul_pair(gdn_out.reshape(m, -1), sc_out, w_out, tm=2048, tn=512,
                       out_dtype=F32, name="out_proj")
    h, hn = _res_norm(mix, h, norm_mix_post, norm_mlp_pre)

    hid = _matmul(hn, w_up, tm=2048, tn=512, tk=D_MODEL, out_dtype=BF16, act="relu2", name="mlp_up")
    ff = _matmul(hid, w_down, tm=2048, tn=1024, tk=2048, out_dtype=F32, name="mlp_down")
    return _res_final(ff, h, norm_mlp_post)


def kernel(x, norm_mix_pre, w_in, conv_qkv_w, a_log, dt_bias, gdn_norm_w, conv_sc_w, w_out, norm_mix_post,
           norm_mlp_pre, w_up, w_down, norm_mlp_post):
    bsz, seq, d = x.shape
    h = x.reshape(bsz * seq, d)
    for l in range(norm_mix_pre.shape[0]):
        h = _layer(h, norm_mix_pre[l], w_in[l], conv_qkv_w[l], a_log[l], dt_bias[l], gdn_norm_w[l], conv_sc_w[l],
                   w_out[l], norm_mix_post[l], norm_mlp_pre[l], w_up[l], w_down[l], norm_mlp_post[l],
                   bsz=bsz, seq=seq)
    return h.reshape(bsz, seq, d)
```

```python
import functools

import jax
import jax.numpy as jnp
from jax import lax
from jax.experimental import pallas as pl
from jax.experimental.pallas import tpu as pltpu

F32 = jnp.float32
BF16 = jnp.bfloat16

D_MODEL = 4096
CHUNK = 128
HEAD_DIM = 128
GDN_WIDTH = 2048
GDN_HEADS = GDN_WIDTH // HEAD_DIM
SC_WIDTH = 2048
GDN_CONV = 4
SC_CONV = 3
NORM_EPS = 1e-6
L2_EPS = 1e-6

V7X_VMEM_LIMIT_BYTES = 60 * 1024 * 1024

NT_DIMS = (((1,), (1,)), ((), ()))


def _params(*sem):
    return pltpu.CompilerParams(dimension_semantics=sem, vmem_limit_bytes=V7X_VMEM_LIMIT_BYTES)


def _sigmoid(x):
    return 1.0 / (1.0 + jnp.exp(-x))


def _silu(x):
    half = 0.5 * x
    return half + half * jnp.tanh(half)


def _rmsnorm_kernel(x_ref, g_ref, o_ref):
    x = x_ref[...]
    ms = jnp.mean(x * x, axis=-1, keepdims=True)
    o_ref[...] = (x * lax.rsqrt(ms + NORM_EPS) * g_ref[...]).astype(o_ref.dtype)


def _rmsnorm(x, g, tm=512):
    m, d = x.shape
    return pl.pallas_call(
        _rmsnorm_kernel,
        grid=(m // tm,),
        in_specs=[pl.BlockSpec((tm, d), lambda i: (i, 0)), pl.BlockSpec((1, d), lambda i: (0, 0))],
        out_specs=pl.BlockSpec((tm, d), lambda i: (i, 0)),
        out_shape=jax.ShapeDtypeStruct((m, d), BF16),
        compiler_params=_params("parallel"),
        name="rmsnorm",
    )(x, g.reshape(1, d))


def _mm_kernel(a_ref, w_ref, *refs, nk, act, w_rows_are_outputs, with_cast):
    if with_cast:
        side_ref, o_ref, side_out_ref = refs
        side_out_ref[...] = side_ref[...].astype(side_out_ref.dtype)
    else:
        (o_ref,) = refs

    def part():
        w = w_ref[...].astype(BF16)
        if w_rows_are_outputs:
            return lax.dot_general(a_ref[...], w, NT_DIMS, preferred_element_type=F32)
        return jnp.dot(a_ref[...], w, preferred_element_type=F32)

    if nk == 1:
        acc = part()
        if act == "relu2":
            r = jnp.maximum(acc, 0.0)
            acc = r * r
        o_ref[...] = acc.astype(o_ref.dtype)
        return
    k = pl.program_id(2)

    @pl.when(k == 0)
    def _():
        o_ref[...] = part()

    @pl.when(k > 0)
    def _():
        o_ref[...] += part()


def _a_buffering(nk, prefetch_rows):
    return pl.Buffered(1) if nk == 1 and not prefetch_rows else None


def _matmul(a, w, *, tm, tn, tk, out_dtype, n_out=None, act=None, w_rows_are_outputs=False, cast_rows=None,
            prefetch_rows=False, name="matmul"):
    m, kdim = a.shape
    if n_out is None:
        n_out = w.shape[0] if w_rows_are_outputs else w.shape[1]
    n = n_out
    nk = kdim // tk
    assert nk == 1 or (act is None and out_dtype == F32)
    if w_rows_are_outputs:
        w_spec = pl.BlockSpec((tn, tk), lambda i, j, k: (j, k))
    else:
        w_spec = pl.BlockSpec((tk, tn), lambda i, j, k: (k, j))
    grid = (m // tm, n // tn, nk)
    in_specs = [pl.BlockSpec((tm, tk), lambda i, j, k: (i, k), pipeline_mode=_a_buffering(nk, prefetch_rows)), w_spec]
    out_specs = [pl.BlockSpec((tm, tn), lambda i, j, k: (i, j))]
    out_shape = [jax.ShapeDtypeStruct((m, n), out_dtype)]
    operands = [a, w]
    if cast_rows is not None:
        row0, nrows, slab = cast_rows
        nslab = nrows // slab
        assert nrows % slab == 0 and row0 % slab == 0 and nslab <= grid[0] * grid[1] * grid[2]

        def slab_of(i, j, k):
            return jnp.minimum((i * grid[1] + j) * grid[2] + k, nslab - 1)

        in_specs.append(pl.BlockSpec((slab, w.shape[1]), lambda i, j, k: (row0 // slab + slab_of(i, j, k), 0)))
        out_specs.append(pl.BlockSpec((slab, w.shape[1]), lambda i, j, k: (slab_of(i, j, k), 0)))
        out_shape.append(jax.ShapeDtypeStruct((nrows, w.shape[1]), BF16))
        operands.append(w)
    out = pl.pallas_call(
        functools.partial(_mm_kernel, nk=nk, act=act, w_rows_are_outputs=w_rows_are_outputs,
                          with_cast=cast_rows is not None),
        grid=grid,
        in_specs=in_specs,
        out_specs=out_specs,
        out_shape=out_shape,
        compiler_params=_params(*(("arbitrary",) * 3 if cast_rows is not None else ("parallel", "parallel", "arbitrary"))),
        name=name,
    )(*operands)
    return out if cast_rows is not None else out[0]


def _mm_pair_kernel(a1_ref, a2_ref, w1_ref, w2_ref, o_ref):
    acc = jnp.dot(a1_ref[...], w1_ref[...].astype(BF16), preferred_element_type=F32)
    acc = acc + jnp.dot(a2_ref[...], w2_ref[...].astype(BF16), preferred_element_type=F32)
    o_ref[...] = acc.astype(o_ref.dtype)


def _matmul_pair(a1, a2, w, *, tm, tn, out_dtype, name):
    m, k1 = a1.shape
    _, k2 = a2.shape
    assert k1 == k2 and w.shape[0] == k1 + k2
    n = w.shape[1]
    return pl.pallas_call(
        _mm_pair_kernel,
        grid=(m // tm, n // tn),
        in_specs=[pl.BlockSpec((tm, k1), lambda i, j: (i, 0), pipeline_mode=pl.Buffered(1)),
                  pl.BlockSpec((tm, k2), lambda i, j: (i, 0), pipeline_mode=pl.Buffered(1)),
                  pl.BlockSpec((k1, tn), lambda i, j: (0, j)), pl.BlockSpec((k2, tn), lambda i, j: (1, j))],
        out_specs=pl.BlockSpec((tm, tn), lambda i, j: (i, j)),
        out_shape=jax.ShapeDtypeStruct((m, n), out_dtype),
        compiler_params=_params("parallel", "parallel"),
        name=name,
    )(a1, a2, w, w)


def _gdn_kernel(q_ref, k_ref, v_ref, z_ref, qh_ref, kh_ref, vh_ref, g_ref, b_ref, wq_ref, wk_ref, wv_ref, nw_ref,
                o_ref, s_ref, u_ref, wqd_ref, kdt_ref, qkm_ref, egl_ref, *, hb, nchunk):
    head0 = pl.program_id(1) * hb
    tile = pl.program_id(2)

    @pl.when(tile == 0)
    def _():
        s_ref[...] = jnp.zeros_like(s_ref)

    row = lax.broadcasted_iota(jnp.int32, (CHUNK, CHUNK), 0)
    col = lax.broadcasted_iota(jnp.int32, (CHUNK, CHUNK), 1)
    tri_incl = row >= col
    tri_strict = row > col
    tri_f = tri_incl.astype(F32)
    eye = (row == col).astype(F32)
    shift = (128 - head0) % 128
    nt = (((1,), (1,)), ((), ()))
    keep_halo = jnp.where(tile > 0, 1.0, 0.0)

    heads = range(hb)
    head_lanes = [slice(j * HEAD_DIM, (j + 1) * HEAD_DIM) for j in heads]

    def precompute(c, first_in_tile=False):
        r0 = pl.multiple_of(c * CHUNK, CHUNK)
        rows = pl.ds(r0, CHUNK)

        graw = pltpu.roll(g_ref[0, rows, :], shift, axis=1)
        beta = pltpu.roll(b_ref[0, rows, :], shift, axis=1)
        gc = jnp.dot(tri_f, graw, preferred_element_type=F32, precision=lax.Precision.HIGHEST)
        eg = jnp.exp(gc)
        g_last = gc[CHUNK - 1:CHUNK, :]
        eg_rest = jnp.exp(g_last - gc)
        gc_t = gc.T

        def conv_silu(ref, halo_ref, w_ref, lanes):
            cur = ref[0, rows, lanes]
            w = w_ref[:, lanes]
            acc = cur * w[GDN_CONV - 1:GDN_CONV, :]
            if first_in_tile:
                prev = halo_ref[0, :, lanes] * keep_halo
            else:
                prev = ref[0, pl.ds(pl.multiple_of(r0 - 8, 8), 8), lanes]
            ext = jnp.concatenate([prev, cur], axis=0)
            for t in range(GDN_CONV - 1):
                back = GDN_CONV - 1 - t
                acc = acc + ext[8 - back:8 - back + CHUNK, :] * w[t:t + 1, :]
            return _silu(acc)

        q = [conv_silu(q_ref, qh_ref, wq_ref, head_lanes[j]) for j in heads]
        k = [conv_silu(k_ref, kh_ref, wk_ref, head_lanes[j]) for j in heads]
        v = [conv_silu(v_ref, vh_ref, wv_ref, head_lanes[j]) for j in heads]
        q = [x * (lax.rsqrt(jnp.sum(x * x, axis=-1, keepdims=True) + L2_EPS) * (HEAD_DIM ** -0.5)) for x in q]
        k = [x * lax.rsqrt(jnp.sum(x * x, axis=-1, keepdims=True) + L2_EPS) for x in k]
        kb = [x.astype(BF16) for x in k]
        kk = [lax.dot_general(kb[j], kb[j], nt, preferred_element_type=F32) for j in heads]
        qk = [lax.dot_general(q[j].astype(BF16), kb[j], nt, preferred_element_type=F32) for j in heads]
        yield

        beta_col = [beta[:, j:j + 1] for j in heads]
        decay = [jnp.where(tri_incl, jnp.exp(gc[:, j:j + 1] - gc_t[j:j + 1, :]), 0.0) for j in heads]
        low = [jnp.where(tri_strict, beta_col[j] * kk[j] * decay[j], 0.0) for j in heads]

        inv = [eye - x for x in low]
        pw = low
        for _ in range(CHUNK.bit_length() - 2):
            pwb = [x.astype(BF16) for x in pw]
            pw = [jnp.dot(x, x, preferred_element_type=F32) for x in pwb]
            inv = [inv[j] + jnp.dot(inv[j].astype(BF16), pw[j].astype(BF16), preferred_element_type=F32)
                   for j in heads]
            yield

        rhs = [jnp.concatenate([v[j] * beta_col[j], k[j] * (beta_col[j] * eg[:, j:j + 1])], axis=1) for j in heads]
        sol = [jnp.dot(inv[j].astype(BF16), rhs[j].astype(BF16), preferred_element_type=F32) for j in heads]
        yield
        egl_ref[c] = jnp.broadcast_to(jnp.exp(g_last), (8, 128))
        for j in heads:
            u_ref[j, rows, :] = sol[j][:, :HEAD_DIM]
            wqd_ref[j, c, :CHUNK, :] = sol[j][:, HEAD_DIM:].astype(BF16)
            wqd_ref[j, c, CHUNK:, :] = (q[j] * eg[:, j:j + 1]).astype(BF16)
            kdt_ref[j, c] = (k[j] * eg_rest[:, j:j + 1]).T.astype(BF16)
            qkm_ref[j, rows, :] = (qk[j] * decay[j]).astype(BF16)

    def scan(c):
        r0 = pl.multiple_of(c * CHUNK, CHUNK)
        rows = pl.ds(r0, CHUNK)
        egl = egl_ref[c]
        state = [s_ref[j] for j in heads]
        u = [u_ref[j, rows, :] for j in heads]
        kdt = [kdt_ref[j, c] for j in heads]
        qkm = [qkm_ref[j, rows, :] for j in heads]
        proj = [jnp.dot(wqd_ref[j, c], state[j].astype(BF16), preferred_element_type=F32) for j in heads]
        yield
        v_new = [(u[j] - proj[j][:CHUNK]).astype(BF16) for j in heads]
        for j in heads:
            s_ref[j] = state[j] * egl[0:1, j:j + 1] + jnp.dot(kdt[j], v_new[j], preferred_element_type=F32)
        o = [proj[j][CHUNK:] + jnp.dot(qkm[j], v_new[j], preferred_element_type=F32) for j in heads]
        yield
        for j in heads:
            ms = jnp.mean(o[j] * o[j], axis=-1, keepdims=True)
            z = z_ref[0, rows, head_lanes[j]]
            out = o[j] * lax.rsqrt(ms + NORM_EPS) * nw_ref[...] * _silu(z)
            o_ref[0, rows, head_lanes[j]] = out.astype(o_ref.dtype)

    def interleave(*parts):
        parts = list(parts)
        while parts:
            parts = [p for p in parts if next(p, StopIteration) is not StopIteration]

    def chain(*parts):
        for p in parts:
            yield from p

    assert nchunk % 2 == 0
    interleave(precompute(0, first_in_tile=True), precompute(1))

    def steady(i, carry):
        c = 2 * i
        interleave(precompute(c + 2), precompute(c + 3), chain(scan(c), scan(c + 1)))
        return carry

    lax.fori_loop(0, nchunk // 2 - 1, steady, 0)
    interleave(chain(scan(nchunk - 2), scan(nchunk - 1)))


def _gdn(proj, g, beta, conv_w, norm_w, *, bsz, seq, hb=8, ts=512):
    nblk = GDN_WIDTH // (HEAD_DIM * hb)
    wblk = HEAD_DIM * hb
    nchunk = ts // CHUNK

    def col(group):
        return pl.BlockSpec((1, ts, wblk), lambda b, h, t: (b, t, group * nblk + h))

    def halo(group):
        return pl.BlockSpec((1, 8, wblk), lambda b, h, t: (b, jnp.maximum(t * (ts // 8) - 1, 0), group * nblk + h))

    def wcol(group):
        return pl.BlockSpec((GDN_CONV, wblk), lambda b, h, t: (0, group * nblk + h))

    gspec = pl.BlockSpec((1, ts, 128), lambda b, h, t: (b, t, 0))
    return pl.pallas_call(
        functools.partial(_gdn_kernel, hb=hb, nchunk=nchunk),
        grid=(bsz, nblk, seq // ts),
        in_specs=[col(0), col(1), col(2), col(3), halo(0), halo(1), halo(2), gspec, gspec,
                  wcol(0), wcol(1), wcol(2), pl.BlockSpec((1, HEAD_DIM), lambda b, h, t: (0, 0))],
        out_specs=pl.BlockSpec((1, ts, wblk), lambda b, h, t: (b, t, h)),
        out_shape=jax.ShapeDtypeStruct((bsz, seq, GDN_WIDTH), BF16),
        scratch_shapes=[pltpu.VMEM((hb, HEAD_DIM, HEAD_DIM), F32),
                        pltpu.VMEM((hb, ts, HEAD_DIM), F32),
                        pltpu.VMEM((hb, nchunk, 2 * CHUNK, HEAD_DIM), BF16),
                        pltpu.VMEM((hb, nchunk, HEAD_DIM, CHUNK), BF16),
                        pltpu.VMEM((hb, ts, CHUNK), BF16),
                        pltpu.VMEM((nchunk, 8, 128), F32)],
        compiler_params=_params("parallel", "parallel", "arbitrary"),
        name="gdn",
    )(proj, proj, proj, proj, proj, proj, proj, g, beta, conv_w, conv_w, conv_w, norm_w)


def _sc_kernel(a_ref, wb0_ref, wb1_ref, wc0_ref, wc1_ref, wh0_ref, wh1_ref, cw_ref, wg_ref, alog_ref, dtb_ref,
               o_ref, g_ref, beta_ref, p_ref, b_ref, *, row_off, rows_per_step, nstep):
    @pl.when(pl.program_id(1) == 0)
    def _():
        p = lax.dot_general(a_ref[...], wg_ref[...], NT_DIMS, preferred_element_type=F32)
        a = p + dtb_ref[...]
        softplus = jnp.maximum(a, 0.0) + jnp.log(1.0 + jnp.exp(-jnp.abs(a)))
        g_ref[...] = -jnp.exp(alog_ref[...]) * softplus
        beta_ref[...] = _sigmoid(pltpu.roll(p, 128 - GDN_HEADS, axis=1))

    def proj(w0_ref, w1_ref):
        w_t = jnp.concatenate([w0_ref[row_off:, :], w1_ref[...].astype(BF16)], axis=0)
        return lax.dot_general(a_ref[...], w_t, NT_DIMS, preferred_element_type=F32)

    p_ref[0:8, :] = jnp.zeros((8, p_ref.shape[1]), F32)
    p_ref[8:, :] = proj(wc0_ref, wc1_ref)
    p_ref[8:, :] *= proj(wh0_ref, wh1_ref)
    b_ref[...] = proj(wb0_ref, wb1_ref)
    w = cw_ref[...]

    def step(i, carry):
        r0 = pl.multiple_of(i * rows_per_step, rows_per_step)
        ext = p_ref[pl.ds(r0, rows_per_step + 8), :]
        acc = ext[8:, :] * w[SC_CONV - 1:SC_CONV, :]
        for t in range(SC_CONV - 1):
            off = 8 - (SC_CONV - 1) + t
            acc = acc + ext[off:off + rows_per_step, :] * w[t:t + 1, :]
        rows = pl.ds(r0, rows_per_step)
        o_ref[rows, :] = (b_ref[rows, :] * acc).astype(o_ref.dtype)
        return carry

    lax.fori_loop(0, nstep, step, 0)


def _sc_proj_conv(xn, w_tail_b, w_t, conv_w, a_log, dt_bias, *, seq, row0, row_off, tc=256, rows_per_step=128):
    m, d = xn.shape
    nblk = SC_WIDTH // tc
    sub = tc // row_off
    assert w_tail_b.shape[0] == 3 * SC_WIDTH and row0 % row_off == 0

    def wrows(group):
        lead = pl.BlockSpec((tc, d), lambda b, j: (group * nblk + j, 0))
        tail = pl.BlockSpec((row_off, d), lambda b, j: (row0 // row_off + (group * nblk + j + 1) * sub, 0))
        return [lead, tail]

    return pl.pallas_call(
        functools.partial(_sc_kernel, row_off=row_off, rows_per_step=rows_per_step, nstep=seq // rows_per_step),
        grid=(m // seq, nblk),
        in_specs=[pl.BlockSpec((seq, d), lambda b, j: (b, 0), pipeline_mode=pl.Buffered(1)),
                  *wrows(0), *wrows(1), *wrows(2), pl.BlockSpec((SC_CONV, tc), lambda b, j: (0, j)),
                  pl.BlockSpec((128, d), lambda b, j: (0, 0)),
                  pl.BlockSpec((1, 128), lambda b, j: (0, 0)), pl.BlockSpec((1, 128), lambda b, j: (0, 0))],
        out_specs=[pl.BlockSpec((seq, tc), lambda b, j: (b, j)),
                   pl.BlockSpec((seq, 128), lambda b, j: (b, 0)), pl.BlockSpec((seq, 128), lambda b, j: (b, 0))],
        out_shape=[jax.ShapeDtypeStruct((m, SC_WIDTH), BF16),
                   jax.ShapeDtypeStruct((m, 128), F32), jax.ShapeDtypeStruct((m, 128), F32)],
        scratch_shapes=[pltpu.VMEM((8 + seq, tc), F32), pltpu.VMEM((seq, tc), F32)],
        compiler_params=_params("parallel", "arbitrary"),
        name="sc_proj_conv",
    )(xn, *([w_tail_b, w_t] * 3), conv_w, w_tail_b, a_log, dt_bias)


def _res_norm_kernel(t_ref, r_ref, gpost_ref, gnext_ref, h_ref, hn_ref):
    t = t_ref[...]
    h = r_ref[...] + t * lax.rsqrt(jnp.mean(t * t, axis=-1, keepdims=True) + NORM_EPS) * gpost_ref[...]
    h_ref[...] = h
    hn = h * lax.rsqrt(jnp.mean(h * h, axis=-1, keepdims=True) + NORM_EPS) * gnext_ref[...]
    hn_ref[...] = hn.astype(hn_ref.dtype)


def _res_norm(t, resid, g_post, g_next, tm=256):
    m, d = t.shape
    row = lambda i: (i, 0)
    fixed = lambda i: (0, 0)
    return pl.pallas_call(
        _res_norm_kernel,
        grid=(m // tm,),
        in_specs=[pl.BlockSpec((tm, d), row), pl.BlockSpec((tm, d), row),
                  pl.BlockSpec((1, d), fixed), pl.BlockSpec((1, d), fixed)],
        out_specs=[pl.BlockSpec((tm, d), row), pl.BlockSpec((tm, d), row)],
        out_shape=[jax.ShapeDtypeStruct((m, d), F32), jax.ShapeDtypeStruct((m, d), BF16)],
        compiler_params=_params("parallel"),
        name="res_norm",
    )(t, resid, g_post.reshape(1, d), g_next.reshape(1, d))


def _res_final_kernel(t_ref, r_ref, g_ref, o_ref):
    t = t_ref[...]
    o_ref[...] = r_ref[...] + t * lax.rsqrt(jnp.mean(t * t, axis=-1, keepdims=True) + NORM_EPS) * g_ref[...]


def _res_final(t, resid, g, tm=256):
    m, d = t.shape
    row = lambda i: (i, 0)
    return pl.pallas_call(
        _res_final_kernel,
        grid=(m // tm,),
        in_specs=[pl.BlockSpec((tm, d), row), pl.BlockSpec((tm, d), row), pl.BlockSpec((1, d), lambda i: (0, 0))],
        out_specs=pl.BlockSpec((tm, d), row),
        out_shape=jax.ShapeDtypeStruct((m, d), F32),
        compiler_params=_params("parallel"),
        name="res_final",
    )(t, resid, g.reshape(1, d))


def _layer(h, norm_mix_pre, w_in, conv_qkv_w, a_log, dt_bias, gdn_norm_w, conv_sc_w, w_out,
           norm_mix_post, norm_mlp_pre, w_up, w_down, norm_mlp_post, *, bsz, seq):
    m = bsz * seq
    n_gdn = 4 * GDN_WIDTH
    n_ab = 2 * GDN_HEADS

    xn = _rmsnorm(h, norm_mix_pre)

    w_in_t = w_in.T
    lane_pad = jnp.zeros((1, 128 - GDN_HEADS), F32)
    a_log_p = jnp.concatenate([a_log.reshape(1, -1), lane_pad], axis=1)
    dt_bias_p = jnp.concatenate([dt_bias.reshape(1, -1), lane_pad], axis=1)

    proj, w_tail_b = _matmul(xn, w_in_t, tm=2048, tn=512, tk=D_MODEL, out_dtype=F32, n_out=n_gdn,
                             w_rows_are_outputs=True, cast_rows=(n_gdn, 3 * SC_WIDTH, 128), name="in_proj")
    sc_out, g, beta = _sc_proj_conv(xn, w_tail_b, w_in_t, conv_sc_w, a_log_p, dt_bias_p, seq=seq, row0=n_gdn,
                                    row_off=n_ab)

    proj3 = proj.reshape(bsz, seq, -1)
    gdn_out = _gdn(proj3, g.reshape(bsz, seq, 128), beta.reshape(bsz, seq, 128), conv_qkv_w,
                   gdn_norm_w.reshape(1, HEAD_DIM), bsz=bsz, seq=seq)

    mix = _matmul_pair(gdn_out.reshape(m, -1), sc_out, w_out, tm=2048, tn=512,
                       out_dtype=F32, name="out_proj")
    h, hn = _res_norm(mix, h, norm_mix_post, norm_mlp_pre)

    hid = _matmul(hn, w_up, tm=2048, tn=512, tk=D_MODEL, out_dtype=BF16, act="relu2", prefetch_rows=True,
                  name="mlp_up")
    ff = _matmul(hid, w_down, tm=2048, tn=1024, tk=2048, out_dtype=F32, name="mlp_down")
    return _res_final(ff, h, norm_mlp_post)


def kernel(x, norm_mix_pre, w_in, conv_qkv_w, a_log, dt_bias, gdn_norm_w, conv_sc_w, w_out, norm_mix_post,
           norm_mlp_pre, w_up, w_down, norm_mlp_post):
    bsz, seq, d = x.shape
    h = x.reshape(bsz * seq, d)
    for l in range(norm_mix_pre.shape[0]):
        h = _layer(h, norm_mix_pre[l], w_in[l], conv_qkv_w[l], a_log[l], dt_bias[l], gdn_norm_w[l], conv_sc_w[l],
                   w_out[l], norm_mix_post[l], norm_mlp_pre[l], w_up[l], w_down[l], norm_mlp_post[l],
                   bsz=bsz, seq=seq)
    return h.reshape(bsz, seq, d)
```
